```python
import math
import jax, jax.numpy as jnp
from jax import lax
import numpy as np

D_MODEL = 4096
BATCH = 4
SEQ = 2048
DEPTH = 1
DEC_BATCH = 128
DEC_SEQ = 4
PAST_LEN = 2048
PAGE_SIZE = 128

N_HEADS_A = D_MODEL // 256
DK_A = 64
DV_A = 2 * DK_A
QK_A = N_HEADS_A * 2 * DK_A
WIDTH_A = N_HEADS_A * DV_A
N_HEADS_B = D_MODEL // 256
D_B = 128
WIDTH_B = N_HEADS_B * D_B
D_FF = ((8 * D_MODEL // 3 + 255) // 256) * 256
CONV_W = 3
Q_BLOCK = 128
ALPHA = (2.0 * DEPTH) ** 0.25
BETA = (8.0 * DEPTH) ** -0.25
LN_EPS = 1e-5
RMS_EPS = 1e-5
IN_SIZES = (QK_A, QK_A, WIDTH_A, WIDTH_B, WIDTH_B, WIDTH_B, N_HEADS_B, D_MODEL, D_MODEL)

kernel_name = 'hybrid_diffattn_fox_convffn_step'


def lambda_init(layer):
    return 0.8 - 0.6 * math.exp(-0.3 * layer)


def alibi_slopes(n_heads):
    return jnp.exp2(-8.0 * jnp.arange(1, n_heads + 1, dtype=jnp.float32) / n_heads)


def layer_norm(x, g, b):
    xf = x.astype(jnp.float32)
    mu = jnp.mean(xf, axis=-1, keepdims=True)
    xc = xf - mu
    var = jnp.mean(xc * xc, axis=-1, keepdims=True)
    return (xc * lax.rsqrt(var + LN_EPS) * g + b).astype(x.dtype)


def rms_norm(x, g):
    xf = x.astype(jnp.float32)
    return (xf * lax.rsqrt(jnp.mean(xf * xf, axis=-1, keepdims=True) + RMS_EPS) * g).astype(x.dtype)


def over_query_blocks(fn, q_pos, *qs):
    t = q_pos.shape[0]
    qb = Q_BLOCK if t % Q_BLOCK == 0 else t
    nb = t // qb
    blocks = tuple(jnp.moveaxis(a.reshape(a.shape[0], nb, qb, *a.shape[2:]), 1, 0) for a in qs)
    out = lax.map(fn, (q_pos.reshape(nb, qb),) + blocks)
    out = jnp.moveaxis(out, 0, 1)
    return out.reshape(out.shape[0], t, *out.shape[3:])


def differential_attention(q, k, v, q_pos, k_pos, lam):
    slopes = alibi_slopes(N_HEADS_A)
    scale = DK_A ** -0.5

    def block(args):
        pb, qb = args
        logits = jnp.einsum('nthmd,nshmd->nhmts', qb, k, preferred_element_type=jnp.float32) * scale
        dist = pb[:, None] - k_pos[None, :]
        logits = logits - slopes[:, None, None, None] * dist.astype(jnp.float32)
        logits = jnp.where(dist >= 0, logits, -jnp.inf)
        p = jax.nn.softmax(logits, axis=-1)
        w = p[:, :, 0] - lam * p[:, :, 1]
        return jnp.einsum('nhts,nshe->nthe', w.astype(v.dtype), v)

    return over_query_blocks(block, q_pos, q)


def forgetting_attention(q, k, v, q_pos, k_pos, c_q, c_k):
    scale = D_B ** -0.5
    c_k_t = jnp.swapaxes(c_k, 1, 2)

    def block(args):
        pb, qb, cb = args
        logits = jnp.einsum('nthd,nshd->nhts', qb, k, preferred_element_type=jnp.float32) * scale
        logits = logits + jnp.swapaxes(cb, 1, 2)[..., None] - c_k_t[:, :, None, :]
        dist = pb[:, None] - k_pos[None, :]
        logits = jnp.where(dist >= 0, logits, -jnp.inf)
        p = jax.nn.softmax(logits, axis=-1)
        return jnp.einsum('nhts,nshd->nthd', p.astype(v.dtype), v)

    return over_query_blocks(block, q_pos, q, c_q)


def decoder_layer(x, past_kA, past_vA, past_kB, past_vB, past_logf, conv_state, layer,
                  w_in, b_f, lambda_q1, lambda_k1, lambda_q2, lambda_k2, subln_g,
                  w_proj_a, w_proj_b, w_out, ln1_g, ln1_b,
                  w_ffn_up, conv_w, conv_b, w_ffn_down, ln2_g, ln2_b):
    n, t, _ = x.shape
    past = past_kA.shape[1]
    offsets = tuple(int(o) for o in np.cumsum(IN_SIZES)[:-1])
    qA, kA, vA, qB, kB, vB, f_logit, gA, gB = jnp.split(x @ w_in, offsets, axis=-1)
    qA = qA.reshape(n, t, N_HEADS_A, 2, DK_A)
    kA = kA.reshape(n, t, N_HEADS_A, 2, DK_A)
    vA = vA.reshape(n, t, N_HEADS_A, DV_A)
    qB = qB.reshape(n, t, N_HEADS_B, D_B)
    kB = kB.reshape(n, t, N_HEADS_B, D_B)
    vB = vB.reshape(n, t, N_HEADS_B, D_B)
    logf = jax.nn.log_sigmoid((f_logit + b_f).astype(jnp.float32))

    q_pos = past + jnp.arange(t)
    k_pos = jnp.arange(past + t)

    lam_init = lambda_init(layer)
    lam = (jnp.exp(jnp.sum(lambda_q1 * lambda_k1).astype(jnp.float32))
           - jnp.exp(jnp.sum(lambda_q2 * lambda_k2).astype(jnp.float32)) + lam_init)
    oA = differential_attention(qA, jnp.concatenate([past_kA, kA], axis=1),
                                jnp.concatenate([past_vA, vA], axis=1), q_pos, k_pos, lam)
    oA = rms_norm(oA, subln_g) * (1.0 - lam_init)

    c_all = jnp.cumsum(jnp.concatenate([past_logf.astype(jnp.float32), logf], axis=1), axis=1)
    oB = forgetting_attention(qB, jnp.concatenate([past_kB, kB], axis=1),
                              jnp.concatenate([past_vB, vB], axis=1), q_pos, k_pos,
                              c_all[:, past:], c_all)

    branch_a = oA.reshape(n, t, WIDTH_A) @ w_proj_a
    branch_b = oB.reshape(n, t, WIDTH_B) @ w_proj_b
    mix = (jax.nn.sigmoid(gA) * branch_a + jax.nn.sigmoid(gB) * branch_b) @ w_out
    h = layer_norm(ALPHA * x + mix, ln1_g, ln1_b)

    up = h @ w_ffn_up
    a, b = up[..., :D_FF], up[..., D_FF:]
    a_ext = jnp.concatenate([conv_state, a], axis=1)
    conv = conv_b + sum(conv_w[j] * a_ext[:, j:j + t] for j in range(CONV_W))
    ffn = (jax.nn.silu(conv) * b) @ w_ffn_down
    y = layer_norm(ALPHA * h + ffn, ln2_g, ln2_b)

    new_state = (kA, vA, kB, vB, logf.astype(x.dtype), a_ext[:, -(CONV_W - 1):])
    return y, new_state


def setup_inputs(seed: int = 0) -> dict:
    key = jax.random.key(seed)
    ks = iter(jax.random.split(key, 48))

    def nrm(shape, scale):
        return jax.random.normal(next(ks), shape, jnp.float32) * scale

    n_pages = PAST_LEN // PAGE_SIZE
    n_used = DEC_BATCH * n_pages
    n_pool = n_used + (n_used + 3) // 4
    s_in = D_MODEL ** -0.5

    x_prompt = nrm((BATCH, SEQ, D_MODEL), 1.0)
    x_sample = nrm((DEC_BATCH, DEC_SEQ, D_MODEL), 1.0)
    cache_diff_k = nrm((DEPTH, n_pool, PAGE_SIZE, N_HEADS_A, 2, DK_A), 1.0)
    cache_diff_v = nrm((DEPTH, n_pool, PAGE_SIZE, N_HEADS_A, DV_A), BETA)
    cache_fox_k = nrm((DEPTH, n_pool, PAGE_SIZE, N_HEADS_B, D_B), 1.0)
    cache_fox_v = nrm((DEPTH, n_pool, PAGE_SIZE, N_HEADS_B, D_B), BETA)
    cache_fox_logf = jax.nn.log_sigmoid(2.0 + nrm((DEPTH, n_pool, PAGE_SIZE, N_HEADS_B), 1.0))
    state_ffn_conv = nrm((DEPTH, DEC_BATCH, CONV_W - 1, D_FF), 1.0)
    page_table = jax.random.permutation(next(ks), n_pool)[:n_used].reshape(DEC_BATCH, n_pages).astype(jnp.int32)

    in_scales = (s_in, s_in, s_in * BETA, s_in, s_in, s_in * BETA, s_in, s_in, s_in)
    w_in = jnp.concatenate([nrm((DEPTH, D_MODEL, w), s) for w, s in zip(IN_SIZES, in_scales)], axis=-1)
    b_f = 2.0 + nrm((DEPTH, N_HEADS_B), 0.5)
    lambda_q1 = nrm((DEPTH, DK_A), 0.1)
    lambda_k1 = nrm((DEPTH, DK_A), 0.1)
    lambda_q2 = nrm((DEPTH, DK_A), 0.1)
    lambda_k2 = nrm((DEPTH, DK_A), 0.1)
    subln_g = 1.0 + nrm((DEPTH, DV_A), 0.01)
    w_proj_a = nrm((DEPTH, WIDTH_A, D_MODEL), WIDTH_A ** -0.5 * BETA)
    w_proj_b = nrm((DEPTH, WIDTH_B, D_MODEL), WIDTH_B ** -0.5 * BETA)
    w_out = nrm((DEPTH, D_MODEL, D_MODEL), s_in * BETA)
    ln1_g = 1.0 + nrm((DEPTH, D_MODEL), 0.01)
    ln1_b = nrm((DEPTH, D_MODEL), 0.01)
    w_ffn_up = nrm((DEPTH, D_MODEL, 2 * D_FF), s_in)
    conv_w = nrm((DEPTH, CONV_W, D_FF), CONV_W ** -0.5)
    conv_b = nrm((DEPTH, D_FF), 0.01)
    w_ffn_down = nrm((DEPTH, D_FF, D_MODEL), D_FF ** -0.5 * BETA)
    ln2_g = 1.0 + nrm((DEPTH, D_MODEL), 0.01)
    ln2_b = nrm((DEPTH, D_MODEL), 0.01)
    return {'x_prompt': x_prompt, 'x_sample': x_sample,
            'cache_diff_k': cache_diff_k, 'cache_diff_v': cache_diff_v,
            'cache_fox_k': cache_fox_k, 'cache_fox_v': cache_fox_v,
            'cache_fox_logf': cache_fox_logf, 'state_ffn_conv': state_ffn_conv,
            'page_table': page_table,
            'w_in': w_in, 'b_f': b_f, 'lambda_q1': lambda_q1, 'lambda_k1': lambda_k1,
            'lambda_q2': lambda_q2, 'lambda_k2': lambda_k2, 'subln_g': subln_g,
            'w_proj_a': w_proj_a, 'w_proj_b': w_proj_b, 'w_out': w_out,
            'ln1_g': ln1_g, 'ln1_b': ln1_b, 'w_ffn_up': w_ffn_up, 'conv_w': conv_w,
            'conv_b': conv_b, 'w_ffn_down': w_ffn_down, 'ln2_g': ln2_g, 'ln2_b': ln2_b}


def reference(x_prompt, x_sample, cache_diff_k, cache_diff_v, cache_fox_k, cache_fox_v,
              cache_fox_logf, state_ffn_conv, page_table,
              w_in, b_f, lambda_q1, lambda_k1, lambda_q2, lambda_k2, subln_g,
              w_proj_a, w_proj_b, w_out, ln1_g, ln1_b,
              w_ffn_up, conv_w, conv_b, w_ffn_down, ln2_g, ln2_b):
    def gather_pages(pool):
        g = pool[page_table]
        return g.reshape(g.shape[0], g.shape[1] * g.shape[2], *g.shape[3:])

    def empty_rows(x, *row_shape):
        return jnp.zeros((x.shape[0], 0) + row_shape, x.dtype)

    y_p, y_s = x_prompt, x_sample
    states_p, states_s = [], []
    for l in range(DEPTH):
        lw = (w_in[l], b_f[l], lambda_q1[l], lambda_k1[l], lambda_q2[l], lambda_k2[l], subln_g[l],
              w_proj_a[l], w_proj_b[l], w_out[l], ln1_g[l], ln1_b[l],
              w_ffn_up[l], conv_w[l], conv_b[l], w_ffn_down[l], ln2_g[l], ln2_b[l])
        y_p, st_p = decoder_layer(
            y_p, empty_rows(y_p, N_HEADS_A, 2, DK_A), empty_rows(y_p, N_HEADS_A, DV_A),
            empty_rows(y_p, N_HEADS_B, D_B), empty_rows(y_p, N_HEADS_B, D_B),
            empty_rows(y_p, N_HEADS_B), jnp.zeros((y_p.shape[0], CONV_W - 1, D_FF), y_p.dtype),
            l, *lw)
        y_s, st_s = decoder_layer(
            y_s, gather_pages(cache_diff_k[l]), gather_pages(cache_diff_v[l]),
            gather_pages(cache_fox_k[l]), gather_pages(cache_fox_v[l]),
            gather_pages(cache_fox_logf[l]), state_ffn_conv[l], l, *lw)
        states_p.append(st_p)
        states_s.append(st_s)

    kA_p, vA_p, kB_p, vB_p, lf_p, conv_p = [jnp.stack([s[i] for s in states_p]) for i in range(6)]
    kA_s, vA_s, kB_s, vB_s, lf_s, conv_s = [jnp.stack([s[i] for s in states_s]) for i in range(6)]
    return (y_p, y_s, kA_p, vA_p, kB_p, vB_p, lf_p, conv_p, kA_s, vA_s, kB_s, vB_s, lf_s, conv_s)
```

```python
import functools
import math

import jax
import jax.numpy as jnp
from jax import lax
from jax.experimental import pallas as pl
from jax.experimental.pallas import tpu as pltpu

F32 = jnp.float32
BF16 = jnp.bfloat16

V7X_VMEM_BYTES = 64 * 1024 * 1024
V7X_VMEM_CAP = 60000 * 1024
LANES = 128
SUBLANES = 8

D_MODEL = 4096
SEQ = 2048
PAGE = 128
N_HEADS = 16
HEAD_W = 128
DK_A = 64
WIDTH = N_HEADS * HEAD_W
D_FF = 11008
DEPTH = 1
ALPHA = (2.0 * DEPTH) ** 0.25
LN_EPS = 1e-5
RMS_EPS = 1e-5
LAM_INIT = 0.8 - 0.6 * math.exp(-0.3 * 0)
NEG_INF = float("-inf")


def _vmem_limit(block_bytes, scratch_bytes=0, temp_bytes=0):
    need = 2 * block_bytes + scratch_bytes + temp_bytes + (2 << 20)
    return int(min(max(need, 16 << 20), V7X_VMEM_CAP))


def _nbytes(shape, dtype):
    return math.prod(shape) * jnp.dtype(dtype).itemsize


def _dot(a, b):
    return jnp.dot(a, b, preferred_element_type=F32)


def _dot_nt(a, b):
    return lax.dot_general(a, b, (((1,), (1,)), ((), ())), preferred_element_type=F32)


def _mm_kernel(*refs, n_extra, n_out, epilogue, cast_w):
    a_ref, w_ref = refs[0], refs[1]
    extra = refs[2:2 + n_extra]
    outs = refs[2 + n_extra:2 + n_extra + n_out]
    if cast_w:
        wbf_ref = refs[2 + n_extra + n_out]

        @pl.when(pl.program_id(1) == 0)
        def _():
            wbf_ref[...] = w_ref[...].astype(BF16)

        w = wbf_ref[...]
    else:
        w = w_ref[...]
    acc = _dot(a_ref[...], w)
    vals = epilogue(acc, *[e[...] for e in extra])
    for o, v in zip(outs, vals):
        o[...] = v.astype(o.dtype)


def _matmul(a, w, *, col0, ncols, tm, tn, out_dtypes, epilogue, extras=(), extra_col0=()):
    m, k = a.shape
    assert m % tm == 0 and ncols % tn == 0 and col0 % tn == 0
    nj, ni = ncols // tn, m // tm
    jb = col0 // tn
    cast_w = w.dtype != BF16
    in_specs = [pl.BlockSpec((tm, k), lambda j, i: (i, 0)),
                pl.BlockSpec((k, tn), lambda j, i: (0, j + jb))]
    block_bytes = _nbytes((tm, k), a.dtype) + _nbytes((k, tn), w.dtype)
    for e, c0 in zip(extras, extra_col0):
        assert c0 % tn == 0
        eb = c0 // tn
        in_specs.append(pl.BlockSpec((tm, tn), lambda j, i, eb=eb: (i, j + eb)))
        block_bytes += _nbytes((tm, tn), e.dtype)
    out_shape = [jax.ShapeDtypeStruct((m, ncols), d) for d in out_dtypes]
    out_specs = [pl.BlockSpec((tm, tn), lambda j, i: (i, j)) for _ in out_dtypes]
    for d in out_dtypes:
        block_bytes += _nbytes((tm, tn), d)
    scratch = [pltpu.VMEM((k, tn), BF16)] if cast_w else []
    scratch_bytes = _nbytes((k, tn), BF16) if cast_w else 0
    kern = functools.partial(_mm_kernel, n_extra=len(extras), n_out=len(out_dtypes),
                             epilogue=epilogue, cast_w=cast_w)
    return pl.pallas_call(
        kern,
        grid=(nj, ni),
        in_specs=in_specs,
        out_specs=out_specs,
        out_shape=out_shape,
        scratch_shapes=scratch,
        compiler_params=pltpu.CompilerParams(
            dimension_semantics=("arbitrary", "arbitrary"),
            vmem_limit_bytes=_vmem_limit(block_bytes, scratch_bytes, 2 * _nbytes((tm, tn), F32))),
    )(a, w, *extras)


def _ep_identity(acc):
    return (acc,)


def _ep_dup(acc):
    return (acc, acc)


def _ep_sigmoid(acc):
    return (jax.nn.sigmoid(acc),)


def _ep_residual(acc, res):
    return (ALPHA * res + acc,)


def _merge_kernel(oa_ref, ob_ref, wa_ref, wb_ref, ga_ref, gb_ref, out_ref, wabf, wbbf):
    @pl.when(pl.program_id(1) == 0)
    def _():
        wabf[...] = wa_ref[...].astype(BF16)
        wbbf[...] = wb_ref[...].astype(BF16)

    ba = _dot(oa_ref[...], wabf[...])
    bb = _dot(ob_ref[...], wbbf[...])
    out_ref[...] = (ga_ref[...] * ba + gb_ref[...] * bb).astype(out_ref.dtype)


def _merge(oa, ob, wa, wb, gates, *, tm, tn):
    m, k = oa.shape
    n = wa.shape[1]
    nj, ni = n // tn, m // tm
    gb0 = n // tn
    block_bytes = (2 * _nbytes((tm, k), BF16) + 2 * _nbytes((k, tn), F32)
                   + 2 * _nbytes((tm, tn), F32) + _nbytes((tm, tn), BF16))
    return pl.pallas_call(
        _merge_kernel,
        grid=(nj, ni),
        in_specs=[pl.BlockSpec((tm, k), lambda j, i: (i, 0)),
                  pl.BlockSpec((tm, k), lambda j, i: (i, 0)),
                  pl.BlockSpec((k, tn), lambda j, i: (0, j)),
                  pl.BlockSpec((k, tn), lambda j, i: (0, j)),
                  pl.BlockSpec((tm, tn), lambda j, i: (i, j)),
                  pl.BlockSpec((tm, tn), lambda j, i: (i, j + gb0))],
        out_specs=pl.BlockSpec((tm, tn), lambda j, i: (i, j)),
        out_shape=jax.ShapeDtypeStruct((m, n), BF16),
        scratch_shapes=[pltpu.VMEM((k, tn), BF16), pltpu.VMEM((k, tn), BF16)],
        compiler_params=pltpu.CompilerParams(
            dimension_semantics=("arbitrary", "arbitrary"),
            vmem_limit_bytes=_vmem_limit(block_bytes, 2 * _nbytes((k, tn), BF16),
                                         4 * _nbytes((tm, tn), F32))),
    )(oa, ob, wa, wb, gates, gates)


def _ln_kernel(x_ref, g_ref, b_ref, of_ref, ob_ref):
    x = x_ref[...]
    mu = jnp.mean(x, axis=-1, keepdims=True)
    xc = x - mu
    var = jnp.mean(xc * xc, axis=-1, keepdims=True)
    y = xc * lax.rsqrt(var + LN_EPS) * g_ref[...] + b_ref[...]
    of_ref[...] = y
    ob_ref[...] = y.astype(BF16)


def _layer_norm(x, g, b, *, tm):
    m, d = x.shape
    block_bytes = _nbytes((tm, d), F32) * 2 + _nbytes((tm, d), BF16)
    return pl.pallas_call(
        _ln_kernel,
        grid=(m // tm,),
        in_specs=[pl.BlockSpec((tm, d), lambda i: (i, 0)),
                  pl.BlockSpec((1, d), lambda i: (0, 0)),
                  pl.BlockSpec((1, d), lambda i: (0, 0))],
        out_specs=[pl.BlockSpec((tm, d), lambda i: (i, 0)),
                   pl.BlockSpec((tm, d), lambda i: (i, 0))],
        out_shape=[jax.ShapeDtypeStruct((m, d), F32), jax.ShapeDtypeStruct((m, d), BF16)],
        compiler_params=pltpu.CompilerParams(
            dimension_semantics=("arbitrary",),
            vmem_limit_bytes=_vmem_limit(block_bytes, 0, 3 * _nbytes((tm, d), F32))),
    )(x, g.reshape(1, d), b.reshape(1, d))


def _log_sigmoid(x):
    return jnp.minimum(x, 0.0) - jnp.log1p(jnp.exp(-jnp.abs(x)))


def _split3(x):
    hi = x.astype(BF16)
    r1 = x - hi.astype(F32)
    mid = r1.astype(BF16)
    lo = (r1 - mid.astype(F32)).astype(BF16)
    return hi, mid, lo


def _dot_exact01(x, ones_bf16):
    hi, mid, lo = _split3(x)
    return _dot(hi, ones_bf16) + _dot(mid, ones_bf16) + _dot(lo, ones_bf16)


def _fgate_kernel(x_ref, wf_ref, wft_ref, bf_row_ref, bf_col_ref, u_ref,
                  logf_ref, csum_ref, carry_ref, *, carry_tiles):
    x = x_ref[...]
    f_nat = _dot(x, wf_ref[...])
    logf_ref[...] = _log_sigmoid(f_nat[:, :N_HEADS] + bf_row_ref[...])
    f_t = _dot_nt(wft_ref[...], x)
    logf_t = _log_sigmoid(f_t + bf_col_ref[...])
    c = _dot_exact01(logf_t, u_ref[...])
    if carry_tiles:
        t = pl.program_id(1)

        @pl.when(t == 0)
        def _():
            carry_ref[...] = jnp.zeros_like(carry_ref)

        c = c + carry_ref[:, 0:1]
        carry_ref[...] = jnp.broadcast_to(c[:, -1:], carry_ref.shape)
    csum_ref[...] = c


def _fgate(x_bf, w_f, b_f, u_mat, *, n_seq, tiles_per_seq, tm, carry_tiles):
    m, k = x_bf.shape
    wf_pad = jnp.zeros((k, LANES), BF16).at[:, :N_HEADS].set(w_f.astype(BF16))
    wft = w_f.T.astype(BF16)
    bf_row = b_f.reshape(1, N_HEADS)
    bf_col = b_f.reshape(N_HEADS, 1)
    kern = functools.partial(_fgate_kernel, carry_tiles=carry_tiles)
    tps = tiles_per_seq
    block_bytes = (_nbytes((tm, k), BF16) + _nbytes((k, LANES), BF16) + _nbytes((N_HEADS, k), BF16)
                   + _nbytes((tm, tm), BF16) + _nbytes((tm, LANES), F32) + _nbytes((N_HEADS, tm), F32))
    return pl.pallas_call(
        kern,
        grid=(n_seq, tps),
        in_specs=[pl.BlockSpec((tm, k), lambda n, t: (n * tps + t, 0)),
                  pl.BlockSpec((k, LANES), lambda n, t: (0, 0)),
                  pl.BlockSpec((N_HEADS, k), lambda n, t: (0, 0)),
                  pl.BlockSpec((1, N_HEADS), lambda n, t: (0, 0)),
                  pl.BlockSpec((N_HEADS, 1), lambda n, t: (0, 0)),
                  pl.BlockSpec((tm, tm), lambda n, t: (0, 0))],
        out_specs=[pl.BlockSpec((tm, N_HEADS), lambda n, t: (n * tps + t, 0)),
                   pl.BlockSpec((N_HEADS, tm), lambda n, t: (0, n * tps + t))],
        out_shape=[jax.ShapeDtypeStruct((m, N_HEADS), F32),
                   jax.ShapeDtypeStruct((N_HEADS, m), F32)],
        scratch_shapes=[pltpu.VMEM((N_HEADS, LANES), F32)],
        compiler_params=pltpu.CompilerParams(
            dimension_semantics=("arbitrary", "arbitrary"),
            vmem_limit_bytes=_vmem_limit(block_bytes, 0, 8 * _nbytes((tm, LANES), F32))),
    )(x_bf, wf_pad, wft, bf_row, bf_col, u_mat)


def _suffix_kernel(x_ref, m_ref, o_ref):
    o_ref[...] = _dot_exact01(x_ref[...], m_ref[...])


def _page_suffix_sums(logf_pool):
    n_pool = logf_pool.shape[0]
    flat = PAGE * N_HEADS
    x = logf_pool.reshape(n_pool, flat)
    src = jnp.arange(flat)
    dst = jnp.arange(flat)
    src_pos, src_h = src // N_HEADS, src % N_HEADS
    dst_h, dst_pos = dst // PAGE, dst % PAGE
    sel = (src_h[:, None] == dst_h[None, :]) & (src_pos[:, None] >= dst_pos[None, :])
    sel = sel.astype(BF16)
    tp = 256
    assert n_pool % tp == 0
    block_bytes = 2 * _nbytes((tp, flat), F32) + _nbytes((flat, flat), BF16)
    out = pl.pallas_call(
        _suffix_kernel,
        grid=(n_pool // tp,),
        in_specs=[pl.BlockSpec((tp, flat), lambda i: (i, 0)),
                  pl.BlockSpec((flat, flat), lambda i: (0, 0))],
        out_specs=pl.BlockSpec((tp, flat), lambda i: (i, 0)),
        out_shape=jax.ShapeDtypeStruct((n_pool, flat), F32),
        compiler_params=pltpu.CompilerParams(
            dimension_semantics=("arbitrary",),
            vmem_limit_bytes=_vmem_limit(block_bytes, 0, 6 * _nbytes((tp, flat), F32))),
    )(x, sel)
    return out.reshape(n_pool, N_HEADS, PAGE)


TQ = 256
NQB = SEQ // TQ


def _lambda_value(lq1, lk1, lq2, lk2):
    s1 = jnp.sum(lq1[...] * lk1[...], axis=-1, keepdims=True)
    s2 = jnp.sum(lq2[...] * lk2[...], axis=-1, keepdims=True)
    return jnp.exp(s1) - jnp.exp(s2) + LAM_INIT


def _causal_strip():
    r = lax.broadcasted_iota(jnp.int32, (TQ, SEQ), 0)
    j = lax.broadcasted_iota(jnp.int32, (TQ, SEQ), 1)
    return r - j + (SEQ - TQ)


def _diff_prompt_kernel(lq1, lk1, lq2, lk2, g_ref, q_ref, k_ref, v_ref, o_ref, bias_ref):
    h = pl.program_id(1)
    lam = _lambda_value(lq1, lk1, lq2, lk2)
    slope = jnp.exp2(jnp.full((1, 1), -0.5, F32) * (h + 1).astype(F32))
    d = _causal_strip()
    bias_ref[...] = jnp.where(d >= 0, -slope * d.astype(F32), NEG_INF)
    lane = lax.broadcasted_iota(jnp.int32, (TQ, HEAD_W), 1)
    g = g_ref[...]
    for b in range(NQB):
        s_len = (b + 1) * TQ
        q = q_ref[b * TQ:(b + 1) * TQ, :] * jnp.asarray(DK_A ** -0.5, BF16)
        q1 = jnp.where(lane < DK_A, q, jnp.zeros_like(q))
        q2 = jnp.where(lane >= DK_A, q, jnp.zeros_like(q))
        k = k_ref[0:s_len, :]
        v = v_ref[0:s_len, :]
        c0 = (NQB - 1 - b) * TQ
        bias = bias_ref[:, c0:c0 + s_len]
        outs = []
        for qm in (q1, q2):
            s = _dot_nt(qm, k) + bias
            m = jnp.max(s, axis=-1, keepdims=True)
            e = jnp.exp(s - m)
            l = jnp.sum(e, axis=-1, keepdims=True)
            outs.append(_dot(e.astype(BF16), v) / l)
        o = outs[0] - lam * outs[1]
        o = o * lax.rsqrt(jnp.mean(o * o, axis=-1, keepdims=True) + RMS_EPS) * g
        o_ref[b * TQ:(b + 1) * TQ, :] = (o * (1.0 - LAM_INIT)).astype(o_ref.dtype)


def _fox_prompt_kernel(q_ref, k_ref, v_ref, c_ref, o_ref, mask_ref):
    d = _causal_strip()
    mask_ref[...] = jnp.where(d >= 0, 0.0, NEG_INF).astype(F32)
    scale = HEAD_W ** -0.5
    for b in range(NQB):
        s_len = (b + 1) * TQ
        q = q_ref[b * TQ:(b + 1) * TQ, :]
        k = k_ref[0:s_len, :]
        v = v_ref[0:s_len, :]
        c0 = (NQB - 1 - b) * TQ
        bias = mask_ref[:, c0:c0 + s_len] - c_ref[:, 0:s_len]
        s = _dot_nt(q, k) * scale + bias
        m = jnp.max(s, axis=-1, keepdims=True)
        e = jnp.exp(s - m)
        l = jnp.sum(e, axis=-1, keepdims=True)
        o_ref[b * TQ:(b + 1) * TQ, :] = (_dot(e.astype(BF16), v) / l).astype(o_ref.dtype)


def _head_spec():
    return pl.BlockSpec((SEQ, HEAD_W), lambda n, h: (n, h))


def _small_spec(shape):
    return pl.BlockSpec(shape, lambda n, h: (0,) * len(shape))


_ATT_TEMP = 6 * TQ * SEQ * 4


def _diff_prompt(q, k, v, lams, subln_g, n_batch):
    block_bytes = 4 * _nbytes((SEQ, HEAD_W), BF16)
    return pl.pallas_call(
        _diff_prompt_kernel,
        grid=(n_batch, N_HEADS),
        in_specs=[_small_spec((1, DK_A))] * 4 + [_small_spec((1, HEAD_W))] + [_head_spec()] * 3,
        out_specs=_head_spec(),
        out_shape=jax.ShapeDtypeStruct(q.shape, BF16),
        scratch_shapes=[pltpu.VMEM((TQ, SEQ), F32)],
        compiler_params=pltpu.CompilerParams(
            dimension_semantics=("arbitrary", "arbitrary"),
            vmem_limit_bytes=_vmem_limit(block_bytes, _nbytes((TQ, SEQ), F32), _ATT_TEMP)),
    )(*lams, subln_g.reshape(1, HEAD_W), q, k, v)


def _fox_prompt(q, k, v, csum, n_batch):
    block_bytes = 4 * _nbytes((SEQ, HEAD_W), BF16) + _nbytes((8, SEQ), F32)
    return pl.pallas_call(
        _fox_prompt_kernel,
        grid=(n_batch, N_HEADS),
        in_specs=[_head_spec()] * 3 + [pl.BlockSpec((None, None, 1, SEQ), lambda n, h: (n, h, 0, 0))],
        out_specs=_head_spec(),
        out_shape=jax.ShapeDtypeStruct(q.shape, BF16),
        scratch_shapes=[pltpu.VMEM((TQ, SEQ), F32)],
        compiler_params=pltpu.CompilerParams(
            dimension_semantics=("arbitrary", "arbitrary"),
            vmem_limit_bytes=_vmem_limit(block_bytes, _nbytes((TQ, SEQ), F32), _ATT_TEMP)),
    )(q, k, v, csum)


N_PAGES = SEQ // PAGE
DEC_T = 4
ROWS = N_HEADS * SUBLANES


def _expand_heads(x):
    return jnp.concatenate(
        [jnp.broadcast_to(x[h:h + 1, :], (SUBLANES, x.shape[1])) for h in range(N_HEADS)], axis=0)


def _online_update(s, v_of_head, m_ref, l_ref, acc_ref):
    m_old = m_ref[...]
    m_new = jnp.maximum(m_old, jnp.max(s, axis=-1, keepdims=True))
    e = jnp.exp(s - m_new)
    alpha = jnp.exp(m_old - m_new)
    l_ref[...] = alpha * l_ref[...] + jnp.sum(e, axis=-1, keepdims=True)
    pv = jnp.concatenate(
        [_dot(e[h * SUBLANES:(h + 1) * SUBLANES, :], v_of_head(h)) for h in range(N_HEADS)], axis=0)
    acc_ref[...] = alpha * acc_ref[...] + pv
    m_ref[...] = m_new


def _head_logits(q_ref, k_of_head):
    return jnp.concatenate(
        [_dot_nt(q_ref[h * SUBLANES:(h + 1) * SUBLANES, :], k_of_head(h)) for h in range(N_HEADS)],
        axis=0)


def _decode_kernel(pt_ref, lq1, lk1, lq2, lk2, g_ref, qa_ref, qb_ref,
                   ka_ref, va_ref, kb_ref, vb_ref, suf_ref,
                   kna_ref, vna_ref, knb_ref, vnb_ref, cnew_ref,
                   oa_ref, ob_ref,
                   ma, la, acca, mb, lb, accb, carry, pka, pva, pkb, pvb):
    del pt_ref
    b = pl.program_id(0)
    step = pl.program_id(1)
    row = lax.broadcasted_iota(jnp.int32, (ROWS, LANES), 0)
    lane = lax.broadcasted_iota(jnp.int32, (ROWS, LANES), 1)
    head_f = (row // SUBLANES + 1).astype(F32)
    t_row = row % DEC_T
    slope = jnp.exp2(-0.5 * head_f)
    fox_scale = HEAD_W ** -0.5

    @pl.when((b == 0) & (step == 0))
    def _():
        for p in (pka, pva, pkb, pvb):
            p[...] = jnp.zeros_like(p)

    @pl.when(step == 0)
    def _():
        for r in (ma, mb):
            r[...] = jnp.full_like(r, NEG_INF)
        for r in (la, lb, acca, accb, carry):
            r[...] = jnp.zeros_like(r)

    def head_slab(ref):
        return lambda h: ref[:, h * HEAD_W:(h + 1) * HEAD_W]

    @pl.when(step < N_PAGES)
    def _():
        page = N_PAGES - 1 - step
        key_pos = (page * PAGE).astype(F32) + lane.astype(F32)
        bias_a = -slope * ((SEQ + t_row).astype(F32) - key_pos)
        sa = _head_logits(qa_ref, head_slab(ka_ref)) + bias_a
        _online_update(sa, head_slab(va_ref), ma, la, acca)

        suf = suf_ref[...]
        lane16 = lax.broadcasted_iota(jnp.int32, (N_HEADS, LANES), 1)
        excl = jnp.where(lane16 == LANES - 1, 0.0, pltpu.roll(suf, LANES - 1, axis=1))
        bias_b = _expand_heads(excl) + carry[...]
        sb = _head_logits(qb_ref, head_slab(kb_ref)) * fox_scale + bias_b
        _online_update(sb, head_slab(vb_ref), mb, lb, accb)
        carry[...] = carry[...] + _expand_heads(jnp.broadcast_to(suf[:, 0:1], (N_HEADS, LANES)))

    @pl.when(step == N_PAGES)
    def _():
        pka[0:DEC_T, :] = kna_ref[...]
        pva[0:DEC_T, :] = vna_ref[...]
        pkb[0:DEC_T, :] = knb_ref[...]
        pvb[0:DEC_T, :] = vnb_ref[...]
        visible = (lane <= t_row) & (lane < DEC_T)
        bias_a = jnp.where(visible, -slope * (t_row - lane).astype(F32), NEG_INF)
        sa = _head_logits(qa_ref, head_slab(pka)) + bias_a
        _online_update(sa, head_slab(pva), ma, la, acca)
        bias_b = jnp.where(visible, -_expand_heads(cnew_ref[...]), NEG_INF)
        sb = _head_logits(qb_ref, head_slab(pkb)) * fox_scale + bias_b
        _online_update(sb, head_slab(pvb), mb, lb, accb)

        lam = _lambda_value(lq1, lk1, lq2, lk2)
        na = acca[...] / la[...]
        o = na - lam * pltpu.roll(na, ROWS - DEC_T, axis=0)
        o = o * lax.rsqrt(jnp.mean(o * o, axis=-1, keepdims=True) + RMS_EPS) * g_ref[...]
        o = o * (1.0 - LAM_INIT)
        nb = accb[...] / lb[...]
        for h in range(N_HEADS):
            oa_ref[:, h * HEAD_W:(h + 1) * HEAD_W] = o[h * SUBLANES:h * SUBLANES + DEC_T, :]
            ob_ref[:, h * HEAD_W:(h + 1) * HEAD_W] = nb[h * SUBLANES:h * SUBLANES + DEC_T, :]


def _decode_attention(page_table, lams, subln_g, qa8, qb8, pools, suffix, new_kv, cnew):
    n_seq = page_table.shape[0]

    def page_map(b, s, pt):
        return (pt[b, jnp.maximum(N_PAGES - 1 - s, 0)], 0, 0)

    def seq_map(b, s, pt):
        return (b, 0, 0)

    def const_map(b, s, pt):
        return (0, 0)

    page_spec = pl.BlockSpec((None, PAGE, WIDTH), page_map)
    in_specs = ([pl.BlockSpec((1, DK_A), const_map)] * 4 + [pl.BlockSpec((1, HEAD_W), const_map)]
                + [pl.BlockSpec((None, ROWS, HEAD_W), seq_map)] * 2
                + [page_spec] * 4
                + [pl.BlockSpec((None, N_HEADS, PAGE), page_map)]
                + [pl.BlockSpec((None, DEC_T, WIDTH), seq_map)] * 4
                + [pl.BlockSpec((None, N_HEADS, LANES), seq_map)])
    out_spec = pl.BlockSpec((None, DEC_T, WIDTH), seq_map)
    scratch = ([pltpu.VMEM((ROWS, 1), F32), pltpu.VMEM((ROWS, 1), F32), pltpu.VMEM((ROWS, HEAD_W), F32)] * 2
               + [pltpu.VMEM((ROWS, LANES), F32)]
               + [pltpu.VMEM((PAGE, WIDTH), F32)] * 4)
    block_bytes = 4 * _nbytes((PAGE, WIDTH), F32) + 12 * _nbytes((ROWS, LANES), F32)
    scratch_bytes = 4 * _nbytes((PAGE, WIDTH), F32) + 8 * _nbytes((ROWS, LANES), F32)
    return pl.pallas_call(
        _decode_kernel,
        grid_spec=pltpu.PrefetchScalarGridSpec(
            num_scalar_prefetch=1,
            grid=(n_seq, N_PAGES + 1),
            in_specs=in_specs,
            out_specs=[out_spec, out_spec],
            scratch_shapes=scratch),
        out_shape=[jax.ShapeDtypeStruct((n_seq, DEC_T, WIDTH), F32)] * 2,
        compiler_params=pltpu.CompilerParams(
            dimension_semantics=("arbitrary", "arbitrary"),
            vmem_limit_bytes=_vmem_limit(block_bytes, scratch_bytes, 64 * _nbytes((ROWS, LANES), F32))),
    )(page_table, *lams, subln_g.reshape(1, HEAD_W), qa8, qb8, *pools, suffix, *new_kv, cnew)


TN_FF = 256


def _conv_prompt_kernel(a_ref, prev_ref, b_ref, cw_ref, cb_ref, o_ref, *, tiles_per_seq):
    t = pl.program_id(1) % tiles_per_seq
    a = a_ref[...]
    tm = a.shape[0]
    prev = jnp.where(t == 0, 0.0, prev_ref[...])
    row8 = lax.broadcasted_iota(jnp.int32, (SUBLANES, a.shape[1]), 0)
    r1 = pltpu.roll(a, 1, axis=0)
    r2 = pltpu.roll(a, 2, axis=0)
    r1_first = jnp.where(row8 < 1, pltpu.roll(prev, 1, axis=0), r1[0:SUBLANES])
    r2_first = jnp.where(row8 < 2, pltpu.roll(prev, 2, axis=0), r2[0:SUBLANES])
    r1 = jnp.concatenate([r1_first, r1[SUBLANES:tm]], axis=0)
    r2 = jnp.concatenate([r2_first, r2[SUBLANES:tm]], axis=0)
    cw = cw_ref[...]
    conv = cb_ref[...] + (cw[0:1] * r2 + cw[1:2] * r1 + cw[2:3] * a)
    o_ref[...] = (jax.nn.silu(conv) * b_ref[...]).astype(o_ref.dtype)


def _conv_act_prompt(up, conv_w, conv_b, *, tm):
    m = up.shape[0]
    nj = D_FF // TN_FF
    ni = m // tm
    tps = SEQ // tm
    sub = tm // SUBLANES
    kern = functools.partial(_conv_prompt_kernel, tiles_per_seq=tps)
    block_bytes = 2 * _nbytes((tm, TN_FF), F32) + _nbytes((tm, TN_FF), BF16) + 3 * _nbytes((8, TN_FF), F32)
    return pl.pallas_call(
        kern,
        grid=(nj, ni),
        in_specs=[pl.BlockSpec((tm, TN_FF), lambda j, i: (i, j)),
                  pl.BlockSpec((SUBLANES, TN_FF), lambda j, i: (jnp.maximum(i * sub - 1, 0), j)),
                  pl.BlockSpec((tm, TN_FF), lambda j, i: (i, j + nj)),
                  pl.BlockSpec((3, TN_FF), lambda j, i: (0, j)),
                  pl.BlockSpec((1, TN_FF), lambda j, i: (0, j))],
        out_specs=pl.BlockSpec((tm, TN_FF), lambda j, i: (i, j)),
        out_shape=jax.ShapeDtypeStruct((m, D_FF), BF16),
        compiler_params=pltpu.CompilerParams(
            dimension_semantics=("arbitrary", "arbitrary"),
            vmem_limit_bytes=_vmem_limit(block_bytes, 0, 6 * _nbytes((tm, TN_FF), F32))),
    )(up, up, up, conv_w, conv_b.reshape(1, D_FF))


def _conv_sample_kernel(a_ref, s0_ref, s1_ref, b_ref, cw_ref, cb_ref, o_ref):
    a = a_ref[...]
    t = lax.broadcasted_iota(jnp.int32, a.shape, 0) % DEC_T
    s0, s1 = s0_ref[...], s1_ref[...]
    r1 = jnp.where(t == 0, s1, pltpu.roll(a, 1, axis=0))
    r2 = jnp.where(t == 0, s0, jnp.where(t == 1, s1, pltpu.roll(a, 2, axis=0)))
    cw = cw_ref[...]
    conv = cb_ref[...] + (cw[0:1] * r2 + cw[1:2] * r1 + cw[2:3] * a)
    o_ref[...] = (jax.nn.silu(conv) * b_ref[...]).astype(o_ref.dtype)


def _conv_act_sample(up, state, conv_w, conv_b):
    m = up.shape[0]
    nj = D_FF // TN_FF
    s0 = jnp.repeat(state[:, 0], DEC_T, axis=0)
    s1 = jnp.repeat(state[:, 1], DEC_T, axis=0)
    block_bytes = 4 * _nbytes((m, TN_FF), F32) + _nbytes((m, TN_FF), BF16)
    return pl.pallas_call(
        _conv_sample_kernel,
        grid=(nj,),
        in_specs=[pl.BlockSpec((m, TN_FF), lambda j: (0, j)),
                  pl.BlockSpec((m, TN_FF), lambda j: (0, j)),
                  pl.BlockSpec((m, TN_FF), lambda j: (0, j)),
                  pl.BlockSpec((m, TN_FF), lambda j: (0, j + nj)),
                  pl.BlockSpec((3, TN_FF), lambda j: (0, j)),
                  pl.BlockSpec((1, TN_FF), lambda j: (0, j))],
        out_specs=pl.BlockSpec((m, TN_FF), lambda j: (0, j)),
        out_shape=jax.ShapeDtypeStruct((m, D_FF), BF16),
        compiler_params=pltpu.CompilerParams(
            dimension_semantics=("arbitrary",),
            vmem_limit_bytes=_vmem_limit(block_bytes, 0, 6 * _nbytes((m, TN_FF), F32))),
    )(up, s0, s1, up, conv_w, conv_b.reshape(1, D_FF))


TN = 512


def _trunk(x_f32, x_bf, attn_a, attn_b, gates, w, *, tm, conv_fn):
    gated = _merge(attn_a, attn_b, w["w_proj_a"], w["w_proj_b"], gates, tm=tm, tn=TN)
    (s1,) = _matmul(gated, w["w_out"], col0=0, ncols=D_MODEL, tm=tm, tn=TN, out_dtypes=(F32,),
                    epilogue=_ep_residual, extras=(x_f32,), extra_col0=(0,))
    h_f32, h_bf = _layer_norm(s1, w["ln1_g"], w["ln1_b"], tm=256)
    (up,) = _matmul(h_bf, w["w_ffn_up"], col0=0, ncols=2 * D_FF, tm=tm, tn=TN, out_dtypes=(F32,),
                    epilogue=_ep_identity)
    act = conv_fn(up)
    (s2,) = _matmul(act, w["w_ffn_down_bf"], col0=0, ncols=D_MODEL, tm=tm, tn=TN, out_dtypes=(F32,),
                    epilogue=_ep_residual, extras=(h_f32,), extra_col0=(0,))
    y, _ = _layer_norm(s2, w["ln2_g"], w["ln2_b"], tm=256)
    return y, up


def kernel(x_prompt, x_sample, cache_diff_k, cache_diff_v, cache_fox_k, cache_fox_v, cache_fox_logf,
           state_ffn_conv, page_table, w_in, b_f, lambda_q1, lambda_k1, lambda_q2, lambda_k2, subln_g,
           w_proj_a, w_proj_b, w_out, ln1_g, ln1_b, w_ffn_up, conv_w, conv_b, w_ffn_down, ln2_g, ln2_b):
    n_batch, seq, d_model = x_prompt.shape
    n_seq, dec_t, _ = x_sample.shape
    assert (seq, d_model, dec_t) == (SEQ, D_MODEL, DEC_T) and w_in.shape[0] == DEPTH == 1
    mp, ms = n_batch * seq, n_seq * dec_t

    w_in0 = w_in[0]
    off_f = 6 * WIDTH
    off_g = off_f + N_HEADS
    w_f = w_in0[:, off_f:off_g]
    w_gate = w_in0[:, off_g:]
    w = dict(w_proj_a=w_proj_a[0], w_proj_b=w_proj_b[0], w_out=w_out[0], ln1_g=ln1_g[0], ln1_b=ln1_b[0],
             w_ffn_up=w_ffn_up[0], w_ffn_down_bf=w_ffn_down[0].astype(BF16), ln2_g=ln2_g[0], ln2_b=ln2_b[0])
    lams = tuple(v.reshape(1, DK_A) for v in (lambda_q1[0], lambda_k1[0], lambda_q2[0], lambda_k2[0]))

    xp = x_prompt.reshape(mp, d_model)
    xs = x_sample.reshape(ms, d_model)
    xp_bf, xs_bf = xp.astype(BF16), xs.astype(BF16)

    def project(x_bf, tm, with_bf_copy):
        res = {}
        names = ("qa", "ka", "va", "qb", "kb", "vb")
        for idx, name in enumerate(names):
            if name[0] == "q":
                (res[name],) = _matmul(x_bf, w_in0, col0=idx * WIDTH, ncols=WIDTH, tm=tm, tn=TN,
                                       out_dtypes=(BF16,), epilogue=_ep_identity)
            elif with_bf_copy:
                res[name], res[name + "_bf"] = _matmul(
                    x_bf, w_in0, col0=idx * WIDTH, ncols=WIDTH, tm=tm, tn=TN,
                    out_dtypes=(F32, BF16), epilogue=_ep_dup)
            else:
                (res[name],) = _matmul(x_bf, w_in0, col0=idx * WIDTH, ncols=WIDTH, tm=tm, tn=TN,
                                       out_dtypes=(F32,), epilogue=_ep_identity)
        (res["gates"],) = _matmul(x_bf, w_gate, col0=0, ncols=2 * D_MODEL, tm=tm, tn=TN,
                                  out_dtypes=(F32,), epilogue=_ep_sigmoid)
        return res

    pp = project(xp_bf, 1024, True)
    ps = project(xs_bf, ms, False)

    tmf = 512
    tri = (jnp.arange(tmf)[:, None] <= jnp.arange(tmf)[None, :])
    logf_p, csum_p = _fgate(xp_bf, w_f, b_f[0], tri.astype(BF16), n_seq=n_batch,
                            tiles_per_seq=seq // tmf, tm=tmf, carry_tiles=True)
    grp = jnp.arange(ms) // dec_t
    blockdiag = tri[:ms, :ms] & (grp[:, None] == grp[None, :])
    logf_s, csum_s = _fgate(xs_bf, w_f, b_f[0], blockdiag.astype(BF16), n_seq=1,
                            tiles_per_seq=1, tm=ms, carry_tiles=False)

    oa_p = _diff_prompt(pp["qa"], pp["ka_bf"], pp["va_bf"], lams, subln_g[0], n_batch)
    csum_rows = csum_p.reshape(N_HEADS, n_batch, 1, seq).transpose(1, 0, 2, 3)
    ob_p = _fox_prompt(pp["qb"], pp["kb_bf"], pp["vb_bf"], csum_rows, n_batch)

    n_pool = cache_diff_k.shape[1]
    pools = (cache_diff_k[0].reshape(n_pool, PAGE, WIDTH), cache_diff_v[0].reshape(n_pool, PAGE, WIDTH),
             cache_fox_k[0].reshape(n_pool, PAGE, WIDTH), cache_fox_v[0].reshape(n_pool, PAGE, WIDTH))
    suffix = _page_suffix_sums(cache_fox_logf[0])
    qa = ps["qa"].astype(F32).reshape(n_seq, dec_t, N_HEADS, 2, DK_A) * (DK_A ** -0.5)
    qa = qa.transpose(0, 2, 3, 1, 4)
    eye = jnp.eye(2, dtype=F32)[None, None, :, None, :, None]
    qa8 = (qa[:, :, :, :, None, :] * eye).reshape(n_seq, ROWS, HEAD_W)
    qb = ps["qb"].astype(F32).reshape(n_seq, dec_t, N_HEADS, HEAD_W).transpose(0, 2, 1, 3)
    qb8 = jnp.concatenate([qb, jnp.zeros_like(qb)], axis=2).reshape(n_seq, ROWS, HEAD_W)
    new_kv = tuple(ps[nm].reshape(n_seq, dec_t, WIDTH) for nm in ("ka", "va", "kb", "vb"))
    cnew = csum_s.reshape(N_HEADS, n_seq, dec_t).transpose(1, 0, 2)
    cnew = jnp.zeros((n_seq, N_HEADS, LANES), F32).at[:, :, :dec_t].set(cnew)
    oa_s, ob_s = _decode_attention(page_table, lams, subln_g[0], qa8, qb8, pools, suffix, new_kv, cnew)
    oa_s = oa_s.reshape(ms, WIDTH).astype(BF16)
    ob_s = ob_s.reshape(ms, WIDTH).astype(BF16)

    conv_w0, conv_b0 = conv_w[0], conv_b[0]
    y_p, up_p = _trunk(xp, xp_bf, oa_p, ob_p, pp["gates"], w, tm=512,
                       conv_fn=lambda up: _conv_act_prompt(up, conv_w0, conv_b0, tm=512))
    y_s, up_s = _trunk(xs, xs_bf, oa_s, ob_s, ps["gates"], w, tm=ms,
                       conv_fn=lambda up: _conv_act_sample(up, state_ffn_conv[0], conv_w0, conv_b0))

    def heads(x, n, t, *tail):
        return x.reshape(1, n, t, N_HEADS, *tail)

    conv_p = up_p.reshape(n_batch, seq, 2 * D_FF)[None, :, seq - 2:, :D_FF]
    conv_s = up_s.reshape(n_seq, dec_t, 2 * D_FF)[None, :, dec_t - 2:, :D_FF]
    return (y_p.reshape(n_batch, seq, d_model), y_s.reshape(n_seq, dec_t, d_model),
            heads(pp["ka"], n_batch, seq, 2, DK_A), heads(pp["va"], n_batch, seq, HEAD_W),
            heads(pp["kb"], n_batch, seq, HEAD_W), heads(pp["vb"], n_batch, seq, HEAD_W),
            heads(logf_p, n_batch, seq), conv_p,
            heads(ps["ka"], n_seq, dec_t, 2, DK_A), heads(ps["va"], n_seq, dec_t, HEAD_W),
            heads(ps["kb"], n_seq, dec_t, HEAD_W), heads(ps["vb"], n_seq, dec_t, HEAD_W),
            heads(logf_s, n_seq, dec_t), conv_s)
```

```python
import functools
import math

import jax
import jax.numpy as jnp
from jax import lax
from jax.experimental import pallas as pl
from jax.experimental.pallas import tpu as pltpu

F32 = jnp.float32
BF16 = jnp.bfloat16

V7X_VMEM_CAP = 60000 * 1024
LANES = 128
SUBLANES = 8

D_MODEL = 4096
SEQ = 2048
PAGE = 128
N_HEADS = 16
HEAD_W = 128
DK_A = 64
WIDTH = N_HEADS * HEAD_W
D_FF = 11008
DEPTH = 1
ALPHA = (2.0 * DEPTH) ** 0.25
LN_EPS = 1e-5
RMS_EPS = 1e-5
LAM_INIT = 0.8 - 0.6 * math.exp(-0.3 * 0)
NEG_INF = float("-inf")


def _vmem_limit(block_bytes, scratch_bytes=0, temp_bytes=0):
    need = 2 * block_bytes + scratch_bytes + temp_bytes + (2 << 20)
    return int(min(max(need, 16 << 20), V7X_VMEM_CAP))


def _nbytes(shape, dtype):
    return math.prod(shape) * jnp.dtype(dtype).itemsize


def _dot(a, b):
    return jnp.dot(a, b, preferred_element_type=F32)


def _dot_nt(a, b):
    return lax.dot_general(a, b, (((1,), (1,)), ((), ())), preferred_element_type=F32)


def _mm_kernel(*refs, n_extra, n_out, epilogue, cast_w, w_t):
    a_ref, w_ref = refs[0], refs[1]
    extra = refs[2:2 + n_extra]
    outs = refs[2 + n_extra:2 + n_extra + n_out]
    if cast_w:
        wbf_ref = refs[2 + n_extra + n_out]

        @pl.when(pl.program_id(1) == 0)
        def _():
            wbf_ref[...] = w_ref[...].astype(BF16)

        w = wbf_ref[...]
    else:
        w = w_ref[...]
    acc = _dot_nt(a_ref[...], w) if w_t else _dot(a_ref[...], w)
    vals = epilogue(acc, *[e[...] for e in extra])
    for o, v in zip(outs, vals):
        o[...] = v.astype(o.dtype)


def _matmul(name, a, w, *, col0, ncols, tm, tn, out_dtypes, epilogue, extras=(), extra_col0=(), w_t=False):
    m, k = a.shape
    assert m % tm == 0 and ncols % tn == 0 and col0 % tn == 0
    nj, ni = ncols // tn, m // tm
    jb = col0 // tn
    cast_w = w.dtype != BF16
    w_block = (tn, k) if w_t else (k, tn)
    w_map = (lambda j, i: (j + jb, 0)) if w_t else (lambda j, i: (0, j + jb))
    in_specs = [pl.BlockSpec((tm, k), lambda j, i: (i, 0)), pl.BlockSpec(w_block, w_map)]
    block_bytes = _nbytes((tm, k), a.dtype) + _nbytes(w_block, w.dtype)
    for e, c0 in zip(extras, extra_col0):
        assert c0 % tn == 0
        eb = c0 // tn
        in_specs.append(pl.BlockSpec((tm, tn), lambda j, i, eb=eb: (i, j + eb)))
        block_bytes += _nbytes((tm, tn), e.dtype)
    out_shape = [jax.ShapeDtypeStruct((m, ncols), d) for d in out_dtypes]
    out_specs = [pl.BlockSpec((tm, tn), lambda j, i: (i, j)) for _ in out_dtypes]
    for d in out_dtypes:
        block_bytes += _nbytes((tm, tn), d)
    scratch = [pltpu.VMEM(w_block, BF16)] if cast_w else []
    scratch_bytes = _nbytes(w_block, BF16) if cast_w else 0
    kern = functools.partial(_mm_kernel, n_extra=len(extras), n_out=len(out_dtypes),
                             epilogue=epilogue, cast_w=cast_w, w_t=w_t)
    return pl.pallas_call(
        kern,
        name=name,
        grid=(nj, ni),
        in_specs=in_specs,
        out_specs=out_specs,
        out_shape=out_shape,
        scratch_shapes=scratch,
        compiler_params=pltpu.CompilerParams(
            dimension_semantics=("arbitrary", "arbitrary"),
            vmem_limit_bytes=_vmem_limit(block_bytes, scratch_bytes, 2 * _nbytes((tm, tn), F32))),
    )(a, w, *extras)


def _ep_identity(acc):
    return (acc,)


def _ep_dup(acc):
    return (acc, acc)


def _ep_sigmoid(acc):
    return (jax.nn.sigmoid(acc),)


def _ep_residual(acc, res):
    return (ALPHA * res + acc,)


def _merge_kernel(oa_ref, ob_ref, wa_ref, wb_ref, ga_ref, gb_ref, out_ref, wabf, wbbf):
    @pl.when(pl.program_id(1) == 0)
    def _():
        wabf[...] = wa_ref[...].astype(BF16)
        wbbf[...] = wb_ref[...].astype(BF16)

    ba = _dot(oa_ref[...], wabf[...])
    bb = _dot(ob_ref[...], wbbf[...])
    out_ref[...] = (ga_ref[...] * ba + gb_ref[...] * bb).astype(out_ref.dtype)


def _merge(name, oa, ob, wa, wb, gates, *, tm, tn):
    m, k = oa.shape
    n = wa.shape[1]
    nj, ni = n // tn, m // tm
    gb0 = n // tn
    block_bytes = (2 * _nbytes((tm, k), BF16) + 2 * _nbytes((k, tn), F32)
                   + 2 * _nbytes((tm, tn), F32) + _nbytes((tm, tn), BF16))
    return pl.pallas_call(
        _merge_kernel,
        name=name,
        grid=(nj, ni),
        in_specs=[pl.BlockSpec((tm, k), lambda j, i: (i, 0)),
                  pl.BlockSpec((tm, k), lambda j, i: (i, 0)),
                  pl.BlockSpec((k, tn), lambda j, i: (0, j)),
                  pl.BlockSpec((k, tn), lambda j, i: (0, j)),
                  pl.BlockSpec((tm, tn), lambda j, i: (i, j)),
                  pl.BlockSpec((tm, tn), lambda j, i: (i, j + gb0))],
        out_specs=pl.BlockSpec((tm, tn), lambda j, i: (i, j)),
        out_shape=jax.ShapeDtypeStruct((m, n), BF16),
        scratch_shapes=[pltpu.VMEM((k, tn), BF16), pltpu.VMEM((k, tn), BF16)],
        compiler_params=pltpu.CompilerParams(
            dimension_semantics=("arbitrary", "arbitrary"),
            vmem_limit_bytes=_vmem_limit(block_bytes, 2 * _nbytes((k, tn), BF16),
                                         4 * _nbytes((tm, tn), F32))),
    )(oa, ob, wa, wb, gates, gates)


def _ln_kernel(x_ref, g_ref, b_ref, of_ref, ob_ref):
    x = x_ref[...]
    mu = jnp.mean(x, axis=-1, keepdims=True)
    xc = x - mu
    var = jnp.mean(xc * xc, axis=-1, keepdims=True)
    y = xc * lax.rsqrt(var + LN_EPS) * g_ref[...] + b_ref[...]
    of_ref[...] = y
    ob_ref[...] = y.astype(BF16)


def _layer_norm(name, x, g, b, *, tm):
    m, d = x.shape
    block_bytes = _nbytes((tm, d), F32) * 2 + _nbytes((tm, d), BF16)
    return pl.pallas_call(
        _ln_kernel,
        name=name,
        grid=(m // tm,),
        in_specs=[pl.BlockSpec((tm, d), lambda i: (i, 0)),
                  pl.BlockSpec((1, d), lambda i: (0, 0)),
                  pl.BlockSpec((1, d), lambda i: (0, 0))],
        out_specs=[pl.BlockSpec((tm, d), lambda i: (i, 0)),
                   pl.BlockSpec((tm, d), lambda i: (i, 0))],
        out_shape=[jax.ShapeDtypeStruct((m, d), F32), jax.ShapeDtypeStruct((m, d), BF16)],
        compiler_params=pltpu.CompilerParams(
            dimension_semantics=("arbitrary",),
            vmem_limit_bytes=_vmem_limit(block_bytes, 0, 3 * _nbytes((tm, d), F32))),
    )(x, g.reshape(1, d), b.reshape(1, d))


def _log_sigmoid(x):
    return jnp.minimum(x, 0.0) - jnp.log1p(jnp.exp(-jnp.abs(x)))


def _split3(x):
    hi = x.astype(BF16)
    r1 = x - hi.astype(F32)
    mid = r1.astype(BF16)
    lo = (r1 - mid.astype(F32)).astype(BF16)
    return hi, mid, lo


def _dot_exact01(x, ones_bf16):
    hi, mid, lo = _split3(x)
    return _dot(hi, ones_bf16) + _dot(mid, ones_bf16) + _dot(lo, ones_bf16)


def _fgate_kernel(x_ref, wf_ref, wft_ref, bf_row_ref, bf_col_ref, u_ref,
                  logf_ref, csum_ref, carry_ref, *, carry_tiles):
    x = x_ref[...]
    f_nat = _dot(x, wf_ref[...])
    logf_ref[...] = _log_sigmoid(f_nat[:, :N_HEADS] + bf_row_ref[...])
    f_t = _dot_nt(wft_ref[...], x)
    logf_t = _log_sigmoid(f_t + bf_col_ref[...])
    c = _dot_exact01(logf_t, u_ref[...])
    if carry_tiles:
        t = pl.program_id(1)

        @pl.when(t == 0)
        def _():
            carry_ref[...] = jnp.zeros_like(carry_ref)

        c = c + carry_ref[:, 0:1]
        carry_ref[...] = jnp.broadcast_to(c[:, -1:], carry_ref.shape)
    csum_ref[...] = c


def _fgate(name, x_bf, w_f_t, b_f, u_mat, *, n_seq, tiles_per_seq, tm, carry_tiles):
    m, k = x_bf.shape
    wft = w_f_t.astype(BF16)
    wf_pad = jnp.zeros((k, LANES), BF16).at[:, :N_HEADS].set(wft.T)
    bf_row = b_f.reshape(1, N_HEADS)
    bf_col = b_f.reshape(N_HEADS, 1)
    kern = functools.partial(_fgate_kernel, carry_tiles=carry_tiles)
    tps = tiles_per_seq
    block_bytes = (_nbytes((tm, k), BF16) + _nbytes((k, LANES), BF16) + _nbytes((N_HEADS, k), BF16)
                   + _nbytes((tm, tm), BF16) + _nbytes((tm, LANES), F32) + _nbytes((N_HEADS, tm), F32))
    return pl.pallas_call(
        kern,
        name=name,
        grid=(n_seq, tps),
        in_specs=[pl.BlockSpec((tm, k), lambda n, t: (n * tps + t, 0)),
                  pl.BlockSpec((k, LANES), lambda n, t: (0, 0)),
                  pl.BlockSpec((N_HEADS, k), lambda n, t: (0, 0)),
                  pl.BlockSpec((1, N_HEADS), lambda n, t: (0, 0)),
                  pl.BlockSpec((N_HEADS, 1), lambda n, t: (0, 0)),
                  pl.BlockSpec((tm, tm), lambda n, t: (0, 0))],
        out_specs=[pl.BlockSpec((tm, N_HEADS), lambda n, t: (n * tps + t, 0)),
                   pl.BlockSpec((N_HEADS, tm), lambda n, t: (0, n * tps + t))],
        out_shape=[jax.ShapeDtypeStruct((m, N_HEADS), F32),
                   jax.ShapeDtypeStruct((N_HEADS, m), F32)],
        scratch_shapes=[pltpu.VMEM((N_HEADS, LANES), F32)],
        compiler_params=pltpu.CompilerParams(
            dimension_semantics=("arbitrary", "arbitrary"),
            vmem_limit_bytes=_vmem_limit(block_bytes, 0, 8 * _nbytes((tm, LANES), F32))),
    )(x_bf, wf_pad, wft, bf_row, bf_col, u_mat)


def _suffix_kernel(x_ref, m_ref, o_ref):
    o_ref[...] = _dot_exact01(x_ref[...], m_ref[...])


def _page_suffix_sums(logf_pool):
    n_pool = logf_pool.shape[0]
    rows = n_pool * N_HEADS
    x = jnp.transpose(logf_pool, (0, 2, 1)).reshape(rows, PAGE)
    sel = (jnp.arange(PAGE)[:, None] >= jnp.arange(PAGE)[None, :]).astype(BF16)
    tr = 4096
    assert rows % tr == 0
    block_bytes = 2 * _nbytes((tr, PAGE), F32) + _nbytes((PAGE, PAGE), BF16)
    out = pl.pallas_call(
        _suffix_kernel,
        name="page_suffix_sums",
        grid=(rows // tr,),
        in_specs=[pl.BlockSpec((tr, PAGE), lambda i: (i, 0)),
                  pl.BlockSpec((PAGE, PAGE), lambda i: (0, 0))],
        out_specs=pl.BlockSpec((tr, PAGE), lambda i: (i, 0)),
        out_shape=jax.ShapeDtypeStruct((rows, PAGE), F32),
        compiler_params=pltpu.CompilerParams(
            dimension_semantics=("arbitrary",),
            vmem_limit_bytes=_vmem_limit(block_bytes, 0, 6 * _nbytes((tr, PAGE), F32))),
    )(x, sel)
    return out.reshape(n_pool, N_HEADS, PAGE)


TQ = 256
NQB = SEQ // TQ


def _lambda_value(lq1, lk1, lq2, lk2):
    s1 = jnp.sum(lq1[...] * lk1[...], axis=-1, keepdims=True)
    s2 = jnp.sum(lq2[...] * lk2[...], axis=-1, keepdims=True)
    return jnp.exp(s1) - jnp.exp(s2) + LAM_INIT


def _causal_strip():
    r = lax.broadcasted_iota(jnp.int32, (TQ, SEQ), 0)
    j = lax.broadcasted_iota(jnp.int32, (TQ, SEQ), 1)
    return r - j + (SEQ - TQ)


def _diff_prompt_kernel(lq1, lk1, lq2, lk2, g_ref, q_ref, k_ref, v_ref, o_ref, bias_ref):
    h = pl.program_id(1)
    lam = _lambda_value(lq1, lk1, lq2, lk2)
    slope = jnp.exp2(jnp.full((1, 1), -0.5, F32) * (h + 1).astype(F32))
    d = _causal_strip()
    bias_ref[...] = jnp.where(d >= 0, -slope * d.astype(F32), NEG_INF)
    lane = lax.broadcasted_iota(jnp.int32, (TQ, HEAD_W), 1)
    g = g_ref[...]
    for b in range(NQB):
        s_len = (b + 1) * TQ
        q = q_ref[b * TQ:(b + 1) * TQ, :] * jnp.asarray(DK_A ** -0.5, BF16)
        q1 = jnp.where(lane < DK_A, q, jnp.zeros_like(q))
        q2 = jnp.where(lane >= DK_A, q, jnp.zeros_like(q))
        k = k_ref[0:s_len, :]
        v = v_ref[0:s_len, :]
        c0 = (NQB - 1 - b) * TQ
        bias = bias_ref[:, c0:c0 + s_len]
        outs = []
        for qm in (q1, q2):
            s = _dot_nt(qm, k) + bias
            m = jnp.max(s, axis=-1, keepdims=True)
            e = jnp.exp(s - m)
            l = jnp.sum(e, axis=-1, keepdims=True)
            outs.append(_dot(e.astype(BF16), v) / l)
        o = outs[0] - lam * outs[1]
        o = o * lax.rsqrt(jnp.mean(o * o, axis=-1, keepdims=True) + RMS_EPS) * g
        o_ref[b * TQ:(b + 1) * TQ, :] = (o * (1.0 - LAM_INIT)).astype(o_ref.dtype)


def _fox_prompt_kernel(q_ref, k_ref, v_ref, c_ref, o_ref, mask_ref):
    d = _causal_strip()
    mask_ref[...] = jnp.where(d >= 0, 0.0, NEG_INF).astype(F32)
    scale = HEAD_W ** -0.5
    for b in range(NQB):
        s_len = (b + 1) * TQ
        q = q_ref[b * TQ:(b + 1) * TQ, :]
        k = k_ref[0:s_len, :]
        v = v_ref[0:s_len, :]
        c0 = (NQB - 1 - b) * TQ
        bias = mask_ref[:, c0:c0 + s_len] - c_ref[:, 0:s_len]
        s = _dot_nt(q, k) * scale + bias
        m = jnp.max(s, axis=-1, keepdims=True)
        e = jnp.exp(s - m)
        l = jnp.sum(e, axis=-1, keepdims=True)
        o_ref[b * TQ:(b + 1) * TQ, :] = (_dot(e.astype(BF16), v) / l).astype(o_ref.dtype)


def _head_spec():
    return pl.BlockSpec((SEQ, HEAD_W), lambda n, h: (n, h))


def _small_spec(shape):
    return pl.BlockSpec(shape, lambda n, h: (0,) * len(shape))


_ATT_TEMP = 6 * TQ * SEQ * 4


def _diff_prompt(q, k, v, lams, subln_g, n_batch):
    block_bytes = 4 * _nbytes((SEQ, HEAD_W), BF16)
    return pl.pallas_call(
        _diff_prompt_kernel,
        name="diff_prompt_attention",
        grid=(n_batch, N_HEADS),
        in_specs=[_small_spec((1, DK_A))] * 4 + [_small_spec((1, HEAD_W))] + [_head_spec()] * 3,
        out_specs=_head_spec(),
        out_shape=jax.ShapeDtypeStruct(q.shape, BF16),
        scratch_shapes=[pltpu.VMEM((TQ, SEQ), F32)],
        compiler_params=pltpu.CompilerParams(
            dimension_semantics=("arbitrary", "arbitrary"),
            vmem_limit_bytes=_vmem_limit(block_bytes, _nbytes((TQ, SEQ), F32), _ATT_TEMP)),
    )(*lams, subln_g.reshape(1, HEAD_W), q, k, v)


def _fox_prompt(q, k, v, csum, n_batch):
    block_bytes = 4 * _nbytes((SEQ, HEAD_W), BF16) + _nbytes((8, SEQ), F32)
    return pl.pallas_call(
        _fox_prompt_kernel,
        name="fox_prompt_attention",
        grid=(n_batch, N_HEADS),
        in_specs=[_head_spec()] * 3 + [pl.BlockSpec((None, None, 1, SEQ), lambda n, h: (n, h, 0, 0))],
        out_specs=_head_spec(),
        out_shape=jax.ShapeDtypeStruct(q.shape, BF16),
        scratch_shapes=[pltpu.VMEM((TQ, SEQ), F32)],
        compiler_params=pltpu.CompilerParams(
            dimension_semantics=("arbitrary", "arbitrary"),
            vmem_limit_bytes=_vmem_limit(block_bytes, _nbytes((TQ, SEQ), F32), _ATT_TEMP)),
    )(q, k, v, csum)


N_PAGES = SEQ // PAGE
PPS = 4
N_STEPS = N_PAGES // PPS
DEC_T = 4
ROWS = N_HEADS * SUBLANES


def _expand_heads(x):
    return jnp.concatenate(
        [jnp.broadcast_to(x[h:h + 1, :], (SUBLANES, x.shape[1])) for h in range(N_HEADS)], axis=0)


def _head_rows(x, h):
    return x[h * SUBLANES:(h + 1) * SUBLANES, :]


def _online_update(s, v_of_head, m_ref, l_ref, acc_ref):
    m_old = m_ref[...]
    m_new = jnp.maximum(m_old, jnp.max(s, axis=-1, keepdims=True))
    e = jnp.exp(s - m_new)
    alpha = jnp.exp(m_old - m_new)
    l_ref[...] = alpha * l_ref[...] + jnp.sum(e, axis=-1, keepdims=True)
    pv = jnp.concatenate([_dot(_head_rows(e, h), v_of_head(h)) for h in range(N_HEADS)], axis=0)
    acc_ref[...] = alpha * acc_ref[...] + pv
    m_ref[...] = m_new


def _decode_kernel(pt_ref, lq1, lk1, lq2, lk2, g_ref, qa_ref, qb_ref, *rest):
    del pt_ref
    kta = rest[0:PPS]
    va = rest[PPS:2 * PPS]
    kb = rest[2 * PPS:3 * PPS]
    vb = rest[3 * PPS:4 * PPS]
    suf = rest[4 * PPS:5 * PPS]
    (kna_ref, vna_ref, knb_ref, vnb_ref, cnew_ref, oa_ref, ob_ref,
     ma, la, acca, mb, lb, accb, carry, pka, pva, pkb, pvb) = rest[5 * PPS:]
    b = pl.program_id(0)
    step = pl.program_id(1)
    row = lax.broadcasted_iota(jnp.int32, (ROWS, LANES), 0)
    lane = lax.broadcasted_iota(jnp.int32, (ROWS, LANES), 1)
    head_f = (row // SUBLANES + 1).astype(F32)
    t_row = row % DEC_T
    slope = jnp.exp2(-0.5 * head_f)
    fox_scale = HEAD_W ** -0.5

    @pl.when((b == 0) & (step == 0))
    def _():
        for p in (pka, pva, pkb, pvb):
            p[...] = jnp.zeros_like(p)

    @pl.when(step == 0)
    def _():
        for r in (ma, mb):
            r[...] = jnp.full_like(r, NEG_INF)
        for r in (la, lb, acca, accb, carry):
            r[...] = jnp.zeros_like(r)

    def strided_head(refs):
        return lambda h: jnp.concatenate(
            [r[pl.ds(h, PAGE, stride=N_HEADS), :] for r in refs], axis=0)

    @pl.when(step < N_STEPS)
    def _():
        q_pos = (SEQ + t_row).astype(F32)
        bias_a = []
        for g in range(PPS):
            page = N_PAGES - 1 - (step * PPS + g)
            key_pos = (page * PAGE).astype(F32) + lane.astype(F32)
            bias_a.append(-slope * (q_pos - key_pos))
        sa = jnp.concatenate(
            [_dot(qa_ref[h * SUBLANES:(h + 1) * SUBLANES, :],
                  jnp.concatenate([r[h * HEAD_W:(h + 1) * HEAD_W, :] for r in kta], axis=1))
             for h in range(N_HEADS)], axis=0)
        sa = sa + jnp.concatenate(bias_a, axis=1)
        _online_update(sa, strided_head(va), ma, la, acca)

        lane16 = lax.broadcasted_iota(jnp.int32, (N_HEADS, LANES), 1)
        c = carry[...]
        bias_b = []
        for g in range(PPS):
            sfx = suf[g][...]
            excl = jnp.where(lane16 == LANES - 1, 0.0, pltpu.roll(sfx, LANES - 1, axis=1))
            bias_b.append(_expand_heads(excl) + c)
            c = c + _expand_heads(jnp.broadcast_to(sfx[:, 0:1], (N_HEADS, LANES)))
        carry[...] = c
        k_of_head = strided_head(kb)
        sb = jnp.concatenate(
            [_dot_nt(qb_ref[h * SUBLANES:(h + 1) * SUBLANES, :], k_of_head(h)) for h in range(N_HEADS)],
            axis=0)
        sb = sb * fox_scale + jnp.concatenate(bias_b, axis=1)
        _online_update(sb, strided_head(vb), mb, lb, accb)

    def head_slab(ref):
        return lambda h: ref[:, h * HEAD_W:(h + 1) * HEAD_W]

    def new_logits(q_ref, k_ref):
        k_of_head = head_slab(k_ref)
        return jnp.concatenate(
            [_dot_nt(q_ref[h * SUBLANES:(h + 1) * SUBLANES, :], k_of_head(h)) for h in range(N_HEADS)],
            axis=0)

    @pl.when(step == N_STEPS)
    def _():
        pka[0:DEC_T, :] = kna_ref[...]
        pva[0:DEC_T, :] = vna_ref[...]
        pkb[0:DEC_T, :] = knb_ref[...]
        pvb[0:DEC_T, :] = vnb_ref[...]
        visible = (lane <= t_row) & (lane < DEC_T)
        bias_a = jnp.where(visible, -slope * (t_row - lane).astype(F32), NEG_INF)
        _online_update(new_logits(qa_ref, pka) + bias_a, head_slab(pva), ma, la, acca)
        bias_b = jnp.where(visible, -_expand_heads(cnew_ref[...]), NEG_INF)
        _online_update(new_logits(qb_ref, pkb) * fox_scale + bias_b, head_slab(pvb), mb, lb, accb)

        lam = _lambda_value(lq1, lk1, lq2, lk2)
        na = acca[...] / la[...]
        o = na - lam * pltpu.roll(na, ROWS - DEC_T, axis=0)
        o = o * lax.rsqrt(jnp.mean(o * o, axis=-1, keepdims=True) + RMS_EPS) * g_ref[...]
        o = o * (1.0 - LAM_INIT)
        nb = accb[...] / lb[...]
        for h in range(N_HEADS):
            oa_ref[:, h * HEAD_W:(h + 1) * HEAD_W] = o[h * SUBLANES:h * SUBLANES + DEC_T, :]
            ob_ref[:, h * HEAD_W:(h + 1) * HEAD_W] = nb[h * SUBLANES:h * SUBLANES + DEC_T, :]


def _decode_attention(page_table, lams, subln_g, qa8, qb8, pools, suffix, new_kv, cnew):
    n_seq = page_table.shape[0]

    def page_map(g):
        def index_map(b, s, pt):
            return (pt[b, N_PAGES - 1 - (jnp.minimum(s, N_STEPS - 1) * PPS + g)], 0, 0)
        return index_map

    def seq_map(b, s, pt):
        return (b, 0, 0)

    def const_map(b, s, pt):
        return (0, 0)

    def page_specs(shape):
        return [pl.BlockSpec((None,) + shape, page_map(g)) for g in range(PPS)]

    in_specs = ([pl.BlockSpec((1, DK_A), const_map)] * 4 + [pl.BlockSpec((1, HEAD_W), const_map)]
                + [pl.BlockSpec((None, ROWS, HEAD_W), seq_map)] * 2
                + page_specs((WIDTH, PAGE)) * 4
                + page_specs((N_HEADS, PAGE))
                + [pl.BlockSpec((None, DEC_T, WIDTH), seq_map)] * 4
                + [pl.BlockSpec((None, N_HEADS, LANES), seq_map)])
    out_spec = pl.BlockSpec((None, DEC_T, WIDTH), seq_map)
    scratch = ([pltpu.VMEM((ROWS, 1), F32), pltpu.VMEM((ROWS, 1), F32), pltpu.VMEM((ROWS, HEAD_W), F32)] * 2
               + [pltpu.VMEM((ROWS, LANES), F32)]
               + [pltpu.VMEM((PAGE, WIDTH), F32)] * 4)
    block_bytes = 4 * PPS * _nbytes((PAGE, WIDTH), F32) + 12 * _nbytes((ROWS, LANES), F32)
    scratch_bytes = 4 * _nbytes((PAGE, WIDTH), F32) + 8 * _nbytes((ROWS, LANES), F32)
    page_args = [p for pool in pools for p in [pool] * PPS] + [suffix] * PPS
    return pl.pallas_call(
        _decode_kernel,
        name="paged_decode_attention",
        grid_spec=pltpu.PrefetchScalarGridSpec(
            num_scalar_prefetch=1,
            grid=(n_seq, N_STEPS + 1),
            in_specs=in_specs,
            out_specs=[out_spec, out_spec],
            scratch_shapes=scratch),
        out_shape=[jax.ShapeDtypeStruct((n_seq, DEC_T, WIDTH), F32)] * 2,
        compiler_params=pltpu.CompilerParams(
            dimension_semantics=("arbitrary", "arbitrary"),
            vmem_limit_bytes=_vmem_limit(block_bytes, scratch_bytes,
                                         256 * _nbytes((ROWS, LANES), F32))),
    )(page_table, *lams, subln_g.reshape(1, HEAD_W), qa8, qb8, *page_args, *new_kv, cnew)


TN_FF = 256


def _conv_gate(a, r1, r2, b, cw, cb):
    conv = cb + (cw[0:1] * r2 + cw[1:2] * r1 + cw[2:3] * a)
    return jax.nn.silu(conv) * b


def _cast_weights(wa_ref, wb_ref, wabf, wbbf):
    @pl.when(pl.program_id(1) == 0)
    def _():
        wabf[...] = wa_ref[...].astype(BF16)
        wbbf[...] = wb_ref[...].astype(BF16)


def _ffn_up_prompt_kernel(h_ref, wa_ref, wb_ref, cw_ref, cb_ref, act_ref, tail_ref,
                          wabf, wbbf, prev_ref, *, tiles_per_seq):
    _cast_weights(wa_ref, wb_ref, wabf, wbbf)
    t = pl.program_id(1) % tiles_per_seq
    hv = h_ref[...]
    a = _dot(hv, wabf[...])
    b = _dot(hv, wbbf[...])
    tm = a.shape[0]
    prev = jnp.where(t == 0, 0.0, prev_ref[...])
    row8 = lax.broadcasted_iota(jnp.int32, (SUBLANES, a.shape[1]), 0)
    r1 = pltpu.roll(a, 1, axis=0)
    r2 = pltpu.roll(a, 2, axis=0)
    r1_first = jnp.where(row8 < 1, pltpu.roll(prev, 1, axis=0), r1[0:SUBLANES])
    r2_first = jnp.where(row8 < 2, pltpu.roll(prev, 2, axis=0), r2[0:SUBLANES])
    r1 = jnp.concatenate([r1_first, r1[SUBLANES:tm]], axis=0)
    r2 = jnp.concatenate([r2_first, r2[SUBLANES:tm]], axis=0)
    act_ref[...] = _conv_gate(a, r1, r2, b, cw_ref[...], cb_ref[...]).astype(act_ref.dtype)
    last = a[tm - SUBLANES:tm]
    prev_ref[...] = last
    tail_ref[...] = last


def _ffn_up_sample_kernel(h_ref, wa_ref, wb_ref, cw_ref, cb_ref, s0_ref, s1_ref, act_ref, a_ref,
                          wabf, wbbf):
    _cast_weights(wa_ref, wb_ref, wabf, wbbf)
    hv = h_ref[...]
    a = _dot(hv, wabf[...])
    b = _dot(hv, wbbf[...])
    t = lax.broadcasted_iota(jnp.int32, a.shape, 0) % DEC_T
    s0, s1 = s0_ref[...], s1_ref[...]
    r1 = jnp.where(t == 0, s1, pltpu.roll(a, 1, axis=0))
    r2 = jnp.where(t == 0, s0, jnp.where(t == 1, s1, pltpu.roll(a, 2, axis=0)))
    act_ref[...] = _conv_gate(a, r1, r2, b, cw_ref[...], cb_ref[...]).astype(act_ref.dtype)
    a_ref[...] = a


def _ffn_up_specs(tm, k, nj):
    tn = TN_FF
    return [pl.BlockSpec((tm, k), lambda j, i: (i, 0)),
            pl.BlockSpec((k, tn), lambda j, i: (0, j)),
            pl.BlockSpec((k, tn), lambda j, i: (0, j + nj)),
            pl.BlockSpec((3, tn), lambda j, i: (0, j)),
            pl.BlockSpec((1, tn), lambda j, i: (0, j))]


def _ffn_up_prompt(h_bf, w_up, conv_w, conv_b, *, tm):
    m, k = h_bf.shape
    tn = TN_FF
    nj, ni, tps = D_FF // tn, m // tm, SEQ // tm
    kern = functools.partial(_ffn_up_prompt_kernel, tiles_per_seq=tps)
    block_bytes = (_nbytes((tm, k), BF16) + 2 * _nbytes((k, tn), F32) + _nbytes((tm, tn), BF16)
                   + 5 * _nbytes((SUBLANES, tn), F32))
    scratch_bytes = 2 * _nbytes((k, tn), BF16) + _nbytes((SUBLANES, tn), F32)
    return pl.pallas_call(
        kern,
        name="ffn_up_conv_prompt",
        grid=(nj, ni),
        in_specs=_ffn_up_specs(tm, k, nj),
        out_specs=[pl.BlockSpec((tm, tn), lambda j, i: (i, j)),
                   pl.BlockSpec((SUBLANES, tn), lambda j, i: (i // tps, j))],
        out_shape=[jax.ShapeDtypeStruct((m, D_FF), BF16),
                   jax.ShapeDtypeStruct((m // SEQ * SUBLANES, D_FF), F32)],
        scratch_shapes=[pltpu.VMEM((k, tn), BF16), pltpu.VMEM((k, tn), BF16),
                        pltpu.VMEM((SUBLANES, tn), F32)],
        compiler_params=pltpu.CompilerParams(
            dimension_semantics=("arbitrary", "arbitrary"),
            vmem_limit_bytes=_vmem_limit(block_bytes, scratch_bytes, 8 * _nbytes((tm, tn), F32))),
    )(h_bf, w_up, w_up, conv_w, conv_b.reshape(1, D_FF))


def _ffn_up_sample(h_bf, w_up, conv_w, conv_b, state):
    m, k = h_bf.shape
    tn = TN_FF
    nj = D_FF // tn
    s0 = jnp.repeat(state[:, 0], DEC_T, axis=0)
    s1 = jnp.repeat(state[:, 1], DEC_T, axis=0)
    block_bytes = (_nbytes((m, k), BF16) + 2 * _nbytes((k, tn), F32) + _nbytes((m, tn), BF16)
                   + 3 * _nbytes((m, tn), F32))
    scratch_bytes = 2 * _nbytes((k, tn), BF16)
    tile = pl.BlockSpec((m, tn), lambda j, i: (0, j))
    return pl.pallas_call(
        _ffn_up_sample_kernel,
        name="ffn_up_conv_sample",
        grid=(nj, 1),
        in_specs=_ffn_up_specs(m, k, nj) + [tile, tile],
        out_specs=[tile, tile],
        out_shape=[jax.ShapeDtypeStruct((m, D_FF), BF16), jax.ShapeDtypeStruct((m, D_FF), F32)],
        scratch_shapes=[pltpu.VMEM((k, tn), BF16), pltpu.VMEM((k, tn), BF16)],
        compiler_params=pltpu.CompilerParams(
            dimension_semantics=("arbitrary", "arbitrary"),
            vmem_limit_bytes=_vmem_limit(block_bytes, scratch_bytes, 8 * _nbytes((m, tn), F32))),
    )(h_bf, w_up, w_up, conv_w, conv_b.reshape(1, D_FF), s0, s1)


TN = 512


def _trunk(tag, x_f32, attn_a, attn_b, gates, w, *, tm, ffn_up_fn):
    gated = _merge("merge_" + tag, attn_a, attn_b, w["w_proj_a"], w["w_proj_b"], gates, tm=tm, tn=TN)
    (s1,) = _matmul("out_proj_" + tag, gated, w["w_out"], col0=0, ncols=D_MODEL, tm=tm, tn=TN,
                    out_dtypes=(F32,), epilogue=_ep_residual, extras=(x_f32,), extra_col0=(0,))
    h_f32, h_bf = _layer_norm("ln1_" + tag, s1, w["ln1_g"], w["ln1_b"], tm=256)
    act, a_info = ffn_up_fn(h_bf)
    (s2,) = _matmul("ffn_down_" + tag, act, w["w_ffn_down_bf"], col0=0, ncols=D_MODEL,
                    tm=min(tm, 512), tn=TN, out_dtypes=(F32,), epilogue=_ep_residual,
                    extras=(h_f32,), extra_col0=(0,))
    y, _ = _layer_norm("ln2_" + tag, s2, w["ln2_g"], w["ln2_b"], tm=256)
    return y, a_info


def kernel(x_prompt, x_sample, cache_diff_k, cache_diff_v, cache_fox_k, cache_fox_v, cache_fox_logf,
           state_ffn_conv, page_table, w_in, b_f, lambda_q1, lambda_k1, lambda_q2, lambda_k2, subln_g,
           w_proj_a, w_proj_b, w_out, ln1_g, ln1_b, w_ffn_up, conv_w, conv_b, w_ffn_down, ln2_g, ln2_b):
    n_batch, seq, d_model = x_prompt.shape
    n_seq, dec_t, _ = x_sample.shape
    assert (seq, d_model, dec_t) == (SEQ, D_MODEL, DEC_T) and w_in.shape[0] == DEPTH == 1
    mp, ms = n_batch * seq, n_seq * dec_t

    w_in_t = jnp.transpose(w_in[0], (1, 0))
    off_f = 6 * WIDTH
    off_g = off_f + N_HEADS
    w_f_t = w_in_t[off_f:off_g]
    w_gate_t = w_in_t[off_g:]
    w = dict(w_proj_a=w_proj_a[0], w_proj_b=w_proj_b[0], w_out=w_out[0], ln1_g=ln1_g[0], ln1_b=ln1_b[0],
             w_ffn_down_bf=w_ffn_down[0].astype(BF16), ln2_g=ln2_g[0], ln2_b=ln2_b[0])
    w_up = w_ffn_up[0]
    lams = tuple(v.reshape(1, DK_A) for v in (lambda_q1[0], lambda_k1[0], lambda_q2[0], lambda_k2[0]))

    xp = x_prompt.reshape(mp, d_model)
    xs = x_sample.reshape(ms, d_model)
    xp_bf, xs_bf = xp.astype(BF16), xs.astype(BF16)

    def project(tag, x_bf, tm, with_bf_copy):
        res = {}
        names = ("qa", "ka", "va", "qb", "kb", "vb")
        for idx, name in enumerate(names):
            common = dict(col0=idx * WIDTH, ncols=WIDTH, tm=tm, tn=TN, w_t=True)
            call = "proj_%s_%s" % (name, tag)
            if name[0] == "q":
                (res[name],) = _matmul(call, x_bf, w_in_t, out_dtypes=(BF16,), epilogue=_ep_identity, **common)
            elif with_bf_copy:
                res[name], res[name + "_bf"] = _matmul(call, x_bf, w_in_t, out_dtypes=(F32, BF16),
                                                       epilogue=_ep_dup, **common)
            else:
                (res[name],) = _matmul(call, x_bf, w_in_t, out_dtypes=(F32,), epilogue=_ep_identity, **common)
        (res["gates"],) = _matmul("proj_gates_" + tag, x_bf, w_gate_t, col0=0, ncols=2 * D_MODEL, tm=tm,
                                  tn=TN, out_dtypes=(F32,), epilogue=_ep_sigmoid, w_t=True)
        return res

    pp = project("prompt", xp_bf, 1024, True)
    ps = project("sample", xs_bf, ms, False)

    tmf = 512
    tri = (jnp.arange(tmf)[:, None] <= jnp.arange(tmf)[None, :])
    logf_p, csum_p = _fgate("fgate_prompt", xp_bf, w_f_t, b_f[0], tri.astype(BF16), n_seq=n_batch,
                            tiles_per_seq=seq // tmf, tm=tmf, carry_tiles=True)
    grp = jnp.arange(ms) // dec_t
    blockdiag = tri[:ms, :ms] & (grp[:, None] == grp[None, :])
    logf_s, csum_s = _fgate("fgate_sample", xs_bf, w_f_t, b_f[0], blockdiag.astype(BF16), n_seq=1,
                            tiles_per_seq=1, tm=ms, carry_tiles=False)

    oa_p = _diff_prompt(pp["qa"], pp["ka_bf"], pp["va_bf"], lams, subln_g[0], n_batch)
    csum_rows = csum_p.reshape(N_HEADS, n_batch, 1, seq).transpose(1, 0, 2, 3)
    ob_p = _fox_prompt(pp["qb"], pp["kb_bf"], pp["vb_bf"], csum_rows, n_batch)

    n_pool = cache_diff_k.shape[1]
    kt_a = jnp.transpose(cache_diff_k[0], (0, 2, 3, 4, 1)).reshape(n_pool, WIDTH, PAGE)
    pools = (kt_a, cache_diff_v[0].reshape(n_pool, PAGE * N_HEADS, HEAD_W),
             cache_fox_k[0].reshape(n_pool, PAGE * N_HEADS, HEAD_W),
             cache_fox_v[0].reshape(n_pool, PAGE * N_HEADS, HEAD_W))
    suffix = _page_suffix_sums(cache_fox_logf[0])
    qa = ps["qa"].astype(F32).reshape(n_seq, dec_t, N_HEADS, 2, DK_A) * (DK_A ** -0.5)
    qa = qa.transpose(0, 2, 3, 1, 4)
    eye = jnp.eye(2, dtype=F32)[None, None, :, None, :, None]
    qa8 = (qa[:, :, :, :, None, :] * eye).reshape(n_seq, ROWS, HEAD_W)
    qb = ps["qb"].astype(F32).reshape(n_seq, dec_t, N_HEADS, HEAD_W).transpose(0, 2, 1, 3)
    qb8 = jnp.concatenate([qb, jnp.zeros_like(qb)], axis=2).reshape(n_seq, ROWS, HEAD_W)
    new_kv = tuple(ps[nm].reshape(n_seq, dec_t, WIDTH) for nm in ("ka", "va", "kb", "vb"))
    cnew = csum_s.reshape(N_HEADS, n_seq, dec_t).transpose(1, 0, 2)
    cnew = jnp.zeros((n_seq, N_HEADS, LANES), F32).at[:, :, :dec_t].set(cnew)
    oa_s, ob_s = _decode_attention(page_table, lams, subln_g[0], qa8, qb8, pools, suffix, new_kv, cnew)
    oa_s = oa_s.reshape(ms, WIDTH).astype(BF16)
    ob_s = ob_s.reshape(ms, WIDTH).astype(BF16)

    conv_w0, conv_b0 = conv_w[0], conv_b[0]
    y_p, tail_p = _trunk("prompt", xp, oa_p, ob_p, pp["gates"], w, tm=1024,
                         ffn_up_fn=lambda h: _ffn_up_prompt(h, w_up, conv_w0, conv_b0, tm=1024))
    y_s, a_s = _trunk("sample", xs, oa_s, ob_s, ps["gates"], w, tm=ms,
                      ffn_up_fn=lambda h: _ffn_up_sample(h, w_up, conv_w0, conv_b0, state_ffn_conv[0]))

    def heads(x, n, t, *tail):
        return x.reshape(1, n, t, N_HEADS, *tail)

    conv_p = tail_p.reshape(n_batch, SUBLANES, D_FF)[None, :, SUBLANES - 2:, :]
    conv_s = a_s.reshape(n_seq, dec_t, D_FF)[None, :, dec_t - 2:, :]
    return (y_p.reshape(n_batch, seq, d_model), y_s.reshape(n_seq, dec_t, d_model),
            heads(pp["ka"], n_batch, seq, 2, DK_A), heads(pp["va"], n_batch, seq, HEAD_W),
            heads(pp["kb"], n_batch, seq, HEAD_W), heads(pp["vb"], n_batch, seq, HEAD_W),
            heads(logf_p, n_batch, seq), conv_p,
            heads(ps["ka"], n_seq, dec_t, 2, DK_A), heads(ps["va"], n_seq, dec_t, HEAD_W),
            heads(ps["kb"], n_seq, dec_t, HEAD_W), heads(ps["vb"], n_seq, dec_t, HEAD_W),
            heads(logf_s, n_seq, dec_t), conv_s)
```

```python
import functools
import math

import jax
import jax.numpy as jnp
from jax import lax
from jax.experimental import pallas as pl
from jax.experimental.pallas import tpu as pltpu

F32 = jnp.float32
BF16 = jnp.bfloat16

V7X_VMEM_CAP = 60000 * 1024
LANES = 128
SUBLANES = 8

D_MODEL = 4096
SEQ = 2048
PAGE = 128
N_HEADS = 16
HEAD_W = 128
DK_A = 64
WIDTH = N_HEADS * HEAD_W
D_FF = 11008
DEPTH = 1
ALPHA = (2.0 * DEPTH) ** 0.25
LN_EPS = 1e-5
RMS_EPS = 1e-5
LAM_INIT = 0.8 - 0.6 * math.exp(-0.3 * 0)
NEG_INF = float("-inf")


def _vmem_limit(block_bytes, scratch_bytes=0, temp_bytes=0):
    need = 2 * block_bytes + scratch_bytes + temp_bytes + (2 << 20)
    return int(min(max(need, 16 << 20), V7X_VMEM_CAP))


def _nbytes(shape, dtype):
    return math.prod(shape) * jnp.dtype(dtype).itemsize


def _dot(a, b):
    return jnp.dot(a, b, preferred_element_type=F32)


def _dot_nt(a, b):
    return lax.dot_general(a, b, (((1,), (1,)), ((), ())), preferred_element_type=F32)


def _mm_kernel(*refs, n_extra, n_out, epilogue, cast_w, w_t):
    a_ref, w_ref = refs[0], refs[1]
    extra = refs[2:2 + n_extra]
    outs = refs[2 + n_extra:2 + n_extra + n_out]
    if cast_w:
        wbf_ref = refs[2 + n_extra + n_out]

        @pl.when(pl.program_id(1) == 0)
        def _():
            wbf_ref[...] = w_ref[...].astype(BF16)

        w = wbf_ref[...]
    else:
        w = w_ref[...]
    acc = _dot_nt(a_ref[...], w) if w_t else _dot(a_ref[...], w)
    vals = epilogue(acc, *[e[...] for e in extra])
    for o, v in zip(outs, vals):
        o[...] = v.astype(o.dtype)


def _matmul(name, a, w, *, col0, ncols, tm, tn, out_dtypes, epilogue, extras=(), extra_col0=(), w_t=False):
    m, k = a.shape
    assert m % tm == 0 and ncols % tn == 0 and col0 % tn == 0
    nj, ni = ncols // tn, m // tm
    jb = col0 // tn
    cast_w = w.dtype != BF16
    w_block = (tn, k) if w_t else (k, tn)
    w_map = (lambda j, i: (j + jb, 0)) if w_t else (lambda j, i: (0, j + jb))
    in_specs = [pl.BlockSpec((tm, k), lambda j, i: (i, 0)), pl.BlockSpec(w_block, w_map)]
    block_bytes = _nbytes((tm, k), a.dtype) + _nbytes(w_block, w.dtype)
    for e, c0 in zip(extras, extra_col0):
        assert c0 % tn == 0
        eb = c0 // tn
        in_specs.append(pl.BlockSpec((tm, tn), lambda j, i, eb=eb: (i, j + eb)))
        block_bytes += _nbytes((tm, tn), e.dtype)
    out_shape = [jax.ShapeDtypeStruct((m, ncols), d) for d in out_dtypes]
    out_specs = [pl.BlockSpec((tm, tn), lambda j, i: (i, j)) for _ in out_dtypes]
    for d in out_dtypes:
        block_bytes += _nbytes((tm, tn), d)
    scratch = [pltpu.VMEM(w_block, BF16)] if cast_w else []
    scratch_bytes = _nbytes(w_block, BF16) if cast_w else 0
    kern = functools.partial(_mm_kernel, n_extra=len(extras), n_out=len(out_dtypes),
                             epilogue=epilogue, cast_w=cast_w, w_t=w_t)
    return pl.pallas_call(
        kern,
        name=name,
        grid=(nj, ni),
        in_specs=in_specs,
        out_specs=out_specs,
        out_shape=out_shape,
        scratch_shapes=scratch,
        compiler_params=pltpu.CompilerParams(
            dimension_semantics=("arbitrary", "arbitrary"),
            vmem_limit_bytes=_vmem_limit(block_bytes, scratch_bytes, 2 * _nbytes((tm, tn), F32))),
    )(a, w, *extras)


def _ep_identity(acc):
    return (acc,)


def _ep_dup(acc):
    return (acc, acc)


def _ep_sigmoid(acc):
    return (jax.nn.sigmoid(acc),)


def _ep_residual(acc, res):
    return (ALPHA * res + acc,)


def _merge_kernel(oa_ref, ob_ref, wa_ref, wb_ref, ga_ref, gb_ref, out_ref, wabf, wbbf):
    @pl.when(pl.program_id(1) == 0)
    def _():
        wabf[...] = wa_ref[...].astype(BF16)
        wbbf[...] = wb_ref[...].astype(BF16)

    ba = _dot(oa_ref[...], wabf[...])
    bb = _dot(ob_ref[...], wbbf[...])
    out_ref[...] = (ga_ref[...] * ba + gb_ref[...] * bb).astype(out_ref.dtype)


def _merge(name, oa, ob, wa, wb, gates, *, tm, tn):
    m, k = oa.shape
    n = wa.shape[1]
    nj, ni = n // tn, m // tm
    gb0 = n // tn
    block_bytes = (2 * _nbytes((tm, k), BF16) + 2 * _nbytes((k, tn), F32)
                   + 2 * _nbytes((tm, tn), F32) + _nbytes((tm, tn), BF16))
    return pl.pallas_call(
        _merge_kernel,
        name=name,
        grid=(nj, ni),
        in_specs=[pl.BlockSpec((tm, k), lambda j, i: (i, 0)),
                  pl.BlockSpec((tm, k), lambda j, i: (i, 0)),
                  pl.BlockSpec((k, tn), lambda j, i: (0, j)),
                  pl.BlockSpec((k, tn), lambda j, i: (0, j)),
                  pl.BlockSpec((tm, tn), lambda j, i: (i, j)),
                  pl.BlockSpec((tm, tn), lambda j, i: (i, j + gb0))],
        out_specs=pl.BlockSpec((tm, tn), lambda j, i: (i, j)),
        out_shape=jax.ShapeDtypeStruct((m, n), BF16),
        scratch_shapes=[pltpu.VMEM((k, tn), BF16), pltpu.VMEM((k, tn), BF16)],
        compiler_params=pltpu.CompilerParams(
            dimension_semantics=("arbitrary", "arbitrary"),
            vmem_limit_bytes=_vmem_limit(block_bytes, 2 * _nbytes((k, tn), BF16),
                                         4 * _nbytes((tm, tn), F32))),
    )(oa, ob, wa, wb, gates, gates)


def _ln_kernel(x_ref, g_ref, b_ref, of_ref, ob_ref):
    x = x_ref[...]
    mu = jnp.mean(x, axis=-1, keepdims=True)
    xc = x - mu
    var = jnp.mean(xc * xc, axis=-1, keepdims=True)
    y = xc * lax.rsqrt(var + LN_EPS) * g_ref[...] + b_ref[...]
    of_ref[...] = y
    ob_ref[...] = y.astype(BF16)


def _layer_norm(name, x, g, b, *, tm):
    m, d = x.shape
    block_bytes = _nbytes((tm, d), F32) * 2 + _nbytes((tm, d), BF16)
    return pl.pallas_call(
        _ln_kernel,
        name=name,
        grid=(m // tm,),
        in_specs=[pl.BlockSpec((tm, d), lambda i: (i, 0)),
                  pl.BlockSpec((1, d), lambda i: (0, 0)),
                  pl.BlockSpec((1, d), lambda i: (0, 0))],
        out_specs=[pl.BlockSpec((tm, d), lambda i: (i, 0)),
                   pl.BlockSpec((tm, d), lambda i: (i, 0))],
        out_shape=[jax.ShapeDtypeStruct((m, d), F32), jax.ShapeDtypeStruct((m, d), BF16)],
        compiler_params=pltpu.CompilerParams(
            dimension_semantics=("arbitrary",),
            vmem_limit_bytes=_vmem_limit(block_bytes, 0, 3 * _nbytes((tm, d), F32))),
    )(x, g.reshape(1, d), b.reshape(1, d))


def _log_sigmoid(x):
    return jnp.minimum(x, 0.0) - jnp.log1p(jnp.exp(-jnp.abs(x)))


def _split3(x):
    hi = x.astype(BF16)
    r1 = x - hi.astype(F32)
    mid = r1.astype(BF16)
    lo = (r1 - mid.astype(F32)).astype(BF16)
    return hi, mid, lo


def _dot_exact01(x, ones_bf16):
    hi, mid, lo = _split3(x)
    return _dot(hi, ones_bf16) + _dot(mid, ones_bf16) + _dot(lo, ones_bf16)


def _fgate_kernel(x_ref, wf_ref, wft_ref, bf_row_ref, bf_col_ref, u_ref,
                  logf_ref, csum_ref, carry_ref, *, carry_tiles):
    x = x_ref[...]
    f_nat = _dot(x, wf_ref[...])
    logf_ref[...] = _log_sigmoid(f_nat[:, :N_HEADS] + bf_row_ref[...])
    f_t = _dot_nt(wft_ref[...], x)
    logf_t = _log_sigmoid(f_t + bf_col_ref[...])
    c = _dot_exact01(logf_t, u_ref[...])
    if carry_tiles:
        t = pl.program_id(1)

        @pl.when(t == 0)
        def _():
            carry_ref[...] = jnp.zeros_like(carry_ref)

        c = c + carry_ref[:, 0:1]
        carry_ref[...] = jnp.broadcast_to(c[:, -1:], carry_ref.shape)
    csum_ref[...] = c


def _fgate(name, x_bf, w_f_t, b_f, u_mat, *, n_seq, tiles_per_seq, tm, carry_tiles):
    m, k = x_bf.shape
    wft = w_f_t.astype(BF16)
    wf_pad = jnp.zeros((k, LANES), BF16).at[:, :N_HEADS].set(wft.T)
    bf_row = b_f.reshape(1, N_HEADS)
    bf_col = b_f.reshape(N_HEADS, 1)
    kern = functools.partial(_fgate_kernel, carry_tiles=carry_tiles)
    tps = tiles_per_seq
    block_bytes = (_nbytes((tm, k), BF16) + _nbytes((k, LANES), BF16) + _nbytes((N_HEADS, k), BF16)
                   + _nbytes((tm, tm), BF16) + _nbytes((tm, LANES), F32) + _nbytes((N_HEADS, tm), F32))
    return pl.pallas_call(
        kern,
        name=name,
        grid=(n_seq, tps),
        in_specs=[pl.BlockSpec((tm, k), lambda n, t: (n * tps + t, 0)),
                  pl.BlockSpec((k, LANES), lambda n, t: (0, 0)),
                  pl.BlockSpec((N_HEADS, k), lambda n, t: (0, 0)),
                  pl.BlockSpec((1, N_HEADS), lambda n, t: (0, 0)),
                  pl.BlockSpec((N_HEADS, 1), lambda n, t: (0, 0)),
                  pl.BlockSpec((tm, tm), lambda n, t: (0, 0))],
        out_specs=[pl.BlockSpec((tm, N_HEADS), lambda n, t: (n * tps + t, 0)),
                   pl.BlockSpec((N_HEADS, tm), lambda n, t: (0, n * tps + t))],
        out_shape=[jax.ShapeDtypeStruct((m, N_HEADS), F32),
                   jax.ShapeDtypeStruct((N_HEADS, m), F32)],
        scratch_shapes=[pltpu.VMEM((N_HEADS, LANES), F32)],
        compiler_params=pltpu.CompilerParams(
            dimension_semantics=("arbitrary", "arbitrary"),
            vmem_limit_bytes=_vmem_limit(block_bytes, 0, 8 * _nbytes((tm, LANES), F32))),
    )(x_bf, wf_pad, wft, bf_row, bf_col, u_mat)


def _suffix_kernel(x_ref, m_ref, o_ref):
    o_ref[...] = _dot_exact01(x_ref[...], m_ref[...])


def _page_suffix_sums(logf_pool):
    n_pool = logf_pool.shape[0]
    rows = n_pool * N_HEADS
    x = jnp.transpose(logf_pool, (0, 2, 1)).reshape(rows, PAGE)
    sel = (jnp.arange(PAGE)[:, None] >= jnp.arange(PAGE)[None, :]).astype(BF16)
    tr = 4096
    assert rows % tr == 0
    block_bytes = 2 * _nbytes((tr, PAGE), F32) + _nbytes((PAGE, PAGE), BF16)
    out = pl.pallas_call(
        _suffix_kernel,
        name="page_suffix_sums",
        grid=(rows // tr,),
        in_specs=[pl.BlockSpec((tr, PAGE), lambda i: (i, 0)),
                  pl.BlockSpec((PAGE, PAGE), lambda i: (0, 0))],
        out_specs=pl.BlockSpec((tr, PAGE), lambda i: (i, 0)),
        out_shape=jax.ShapeDtypeStruct((rows, PAGE), F32),
        compiler_params=pltpu.CompilerParams(
            dimension_semantics=("arbitrary",),
            vmem_limit_bytes=_vmem_limit(block_bytes, 0, 6 * _nbytes((tr, PAGE), F32))),
    )(x, sel)
    return out.reshape(n_pool, N_HEADS, PAGE)


TQ = 256
NQB = SEQ // TQ


def _lambda_value(lq1, lk1, lq2, lk2):
    s1 = jnp.sum(lq1[...] * lk1[...], axis=-1, keepdims=True)
    s2 = jnp.sum(lq2[...] * lk2[...], axis=-1, keepdims=True)
    return jnp.exp(s1) - jnp.exp(s2) + LAM_INIT


def _causal_strip():
    r = lax.broadcasted_iota(jnp.int32, (TQ, SEQ), 0)
    j = lax.broadcasted_iota(jnp.int32, (TQ, SEQ), 1)
    return r - j + (SEQ - TQ)


def _diff_prompt_kernel(lq1, lk1, lq2, lk2, g_ref, q_ref, k_ref, v_ref, o_ref, bias_ref):
    h = pl.program_id(1)
    lam = _lambda_value(lq1, lk1, lq2, lk2)
    slope = jnp.exp2(jnp.full((1, 1), -0.5, F32) * (h + 1).astype(F32))
    d = _causal_strip()
    bias_ref[...] = jnp.where(d >= 0, -slope * d.astype(F32), NEG_INF)
    lane = lax.broadcasted_iota(jnp.int32, (TQ, HEAD_W), 1)
    g = g_ref[...]
    for b in range(NQB):
        s_len = (b + 1) * TQ
        q = q_ref[b * TQ:(b + 1) * TQ, :] * jnp.asarray(DK_A ** -0.5, BF16)
        q1 = jnp.where(lane < DK_A, q, jnp.zeros_like(q))
        q2 = jnp.where(lane >= DK_A, q, jnp.zeros_like(q))
        k = k_ref[0:s_len, :]
        v = v_ref[0:s_len, :]
        c0 = (NQB - 1 - b) * TQ
        bias = bias_ref[:, c0:c0 + s_len]
        outs = []
        for qm in (q1, q2):
            s = _dot_nt(qm, k) + bias
            m = jnp.max(s, axis=-1, keepdims=True)
            e = jnp.exp(s - m)
            l = jnp.sum(e, axis=-1, keepdims=True)
            outs.append(_dot(e.astype(BF16), v) / l)
        o = outs[0] - lam * outs[1]
        o = o * lax.rsqrt(jnp.mean(o * o, axis=-1, keepdims=True) + RMS_EPS) * g
        o_ref[b * TQ:(b + 1) * TQ, :] = (o * (1.0 - LAM_INIT)).astype(o_ref.dtype)


def _fox_prompt_kernel(q_ref, k_ref, v_ref, c_ref, o_ref, mask_ref):
    d = _causal_strip()
    mask_ref[...] = jnp.where(d >= 0, 0.0, NEG_INF).astype(F32)
    scale = HEAD_W ** -0.5
    for b in range(NQB):
        s_len = (b + 1) * TQ
        q = q_ref[b * TQ:(b + 1) * TQ, :]
        k = k_ref[0:s_len, :]
        v = v_ref[0:s_len, :]
        c0 = (NQB - 1 - b) * TQ
        bias = mask_ref[:, c0:c0 + s_len] - c_ref[:, 0:s_len]
        s = _dot_nt(q, k) * scale + bias
        m = jnp.max(s, axis=-1, keepdims=True)
        e = jnp.exp(s - m)
        l = jnp.sum(e, axis=-1, keepdims=True)
        o_ref[b * TQ:(b + 1) * TQ, :] = (_dot(e.astype(BF16), v) / l).astype(o_ref.dtype)


def _head_spec():
    return pl.BlockSpec((SEQ, HEAD_W), lambda n, h: (n, h))


def _small_spec(shape):
    return pl.BlockSpec(shape, lambda n, h: (0,) * len(shape))


_ATT_TEMP = 6 * TQ * SEQ * 4


def _diff_prompt(q, k, v, lams, subln_g, n_batch):
    block_bytes = 4 * _nbytes((SEQ, HEAD_W), BF16)
    return pl.pallas_call(
        _diff_prompt_kernel,
        name="diff_prompt_attention",
        grid=(n_batch, N_HEADS),
        in_specs=[_small_spec((1, DK_A))] * 4 + [_small_spec((1, HEAD_W))] + [_head_spec()] * 3,
        out_specs=_head_spec(),
        out_shape=jax.ShapeDtypeStruct(q.shape, BF16),
        scratch_shapes=[pltpu.VMEM((TQ, SEQ), F32)],
        compiler_params=pltpu.CompilerParams(
            dimension_semantics=("arbitrary", "arbitrary"),
            vmem_limit_bytes=_vmem_limit(block_bytes, _nbytes((TQ, SEQ), F32), _ATT_TEMP)),
    )(*lams, subln_g.reshape(1, HEAD_W), q, k, v)


def _fox_prompt(q, k, v, csum, n_batch):
    block_bytes = 4 * _nbytes((SEQ, HEAD_W), BF16) + _nbytes((8, SEQ), F32)
    return pl.pallas_call(
        _fox_prompt_kernel,
        name="fox_prompt_attention",
        grid=(n_batch, N_HEADS),
        in_specs=[_head_spec()] * 3 + [pl.BlockSpec((None, None, 1, SEQ), lambda n, h: (n, h, 0, 0))],
        out_specs=_head_spec(),
        out_shape=jax.ShapeDtypeStruct(q.shape, BF16),
        scratch_shapes=[pltpu.VMEM((TQ, SEQ), F32)],
        compiler_params=pltpu.CompilerParams(
            dimension_semantics=("arbitrary", "arbitrary"),
            vmem_limit_bytes=_vmem_limit(block_bytes, _nbytes((TQ, SEQ), F32), _ATT_TEMP)),
    )(q, k, v, csum)


N_PAGES = SEQ // PAGE
PPS = 4
N_STEPS = N_PAGES // PPS
DEC_T = 4
ROWS = N_HEADS * SUBLANES
HEAD_GROUPS = N_HEADS // SUBLANES


def _expand_heads(x):
    return jnp.concatenate(
        [jnp.broadcast_to(x[h:h + 1, :], (SUBLANES, x.shape[1])) for h in range(N_HEADS)], axis=0)


def _head_rows(x, h):
    return x[h * SUBLANES:(h + 1) * SUBLANES, :]


def _online_update(s, v_of_head, m_ref, l_ref, acc_ref):
    m_old = m_ref[...]
    m_new = jnp.maximum(m_old, jnp.max(s, axis=-1, keepdims=True))
    e = jnp.exp(s - m_new)
    alpha = jnp.exp(m_old - m_new)
    l_ref[...] = alpha * l_ref[...] + jnp.sum(e, axis=-1, keepdims=True)
    pv = jnp.concatenate([_dot(_head_rows(e, h), v_of_head(h)) for h in range(N_HEADS)], axis=0)
    acc_ref[...] = alpha * acc_ref[...] + pv
    m_ref[...] = m_new


def _decode_kernel(pt_ref, lq1, lk1, lq2, lk2, g_ref, qa_ref, qb_ref, *rest):
    del pt_ref
    n_blk = PPS * HEAD_GROUPS
    kta = rest[0:PPS]
    va, kb, vb = (rest[PPS + i * n_blk:PPS + (i + 1) * n_blk] for i in range(3))
    suf = rest[PPS + 3 * n_blk:2 * PPS + 3 * n_blk]
    (kna_ref, vna_ref, knb_ref, vnb_ref, cnew_ref, oa_ref, ob_ref,
     ma, la, acca, mb, lb, accb, carry, pka, pva, pkb, pvb) = rest[2 * PPS + 3 * n_blk:]
    b = pl.program_id(0)
    step = pl.program_id(1)
    row = lax.broadcasted_iota(jnp.int32, (ROWS, LANES), 0)
    lane = lax.broadcasted_iota(jnp.int32, (ROWS, LANES), 1)
    head_f = (row // SUBLANES + 1).astype(F32)
    t_row = row % DEC_T
    slope = jnp.exp2(-0.5 * head_f)
    fox_scale = HEAD_W ** -0.5

    @pl.when((b == 0) & (step == 0))
    def _():
        for p in (pka, pva, pkb, pvb):
            p[...] = jnp.zeros_like(p)

    @pl.when(step == 0)
    def _():
        for r in (ma, mb):
            r[...] = jnp.full_like(r, NEG_INF)
        for r in (la, lb, acca, accb, carry):
            r[...] = jnp.zeros_like(r)

    def strided_head(refs):
        def load(h):
            hg, h8 = divmod(h, SUBLANES)
            return jnp.concatenate(
                [refs[g * HEAD_GROUPS + hg].reshape(PAGE * SUBLANES, HEAD_W)[pl.ds(h8, PAGE, stride=SUBLANES), :]
                 for g in range(PPS)], axis=0)
        return load

    @pl.when(step < N_STEPS)
    def _():
        q_pos = (SEQ + t_row).astype(F32)
        bias_a = []
        for g in range(PPS):
            page = N_PAGES - 1 - (step * PPS + g)
            key_pos = (page * PAGE).astype(F32) + lane.astype(F32)
            bias_a.append(-slope * (q_pos - key_pos))
        sa = jnp.concatenate(
            [_dot(qa_ref[h * SUBLANES:(h + 1) * SUBLANES, :],
                  jnp.concatenate([r[h * HEAD_W:(h + 1) * HEAD_W, :] for r in kta], axis=1))
             for h in range(N_HEADS)], axis=0)
        sa = sa + jnp.concatenate(bias_a, axis=1)
        _online_update(sa, strided_head(va), ma, la, acca)

        lane16 = lax.broadcasted_iota(jnp.int32, (N_HEADS, LANES), 1)
        c = carry[...]
        bias_b = []
        for g in range(PPS):
            sfx = suf[g][...]
            excl = jnp.where(lane16 == LANES - 1, 0.0, pltpu.roll(sfx, LANES - 1, axis=1))
            bias_b.append(_expand_heads(excl) + c)
            c = c + _expand_heads(jnp.broadcast_to(sfx[:, 0:1], (N_HEADS, LANES)))
        carry[...] = c
        k_of_head = strided_head(kb)
        sb = jnp.concatenate(
            [_dot_nt(qb_ref[h * SUBLANES:(h + 1) * SUBLANES, :], k_of_head(h)) for h in range(N_HEADS)],
            axis=0)
        sb = sb * fox_scale + jnp.concatenate(bias_b, axis=1)
        _online_update(sb, strided_head(vb), mb, lb, accb)

    def head_slab(ref):
        return lambda h: ref[:, h * HEAD_W:(h + 1) * HEAD_W]

    def new_logits(q_ref, k_ref):
        k_of_head = head_slab(k_ref)
        return jnp.concatenate(
            [_dot_nt(q_ref[h * SUBLANES:(h + 1) * SUBLANES, :], k_of_head(h)) for h in range(N_HEADS)],
            axis=0)

    @pl.when(step == N_STEPS)
    def _():
        pka[0:DEC_T, :] = kna_ref[...]
        pva[0:DEC_T, :] = vna_ref[...]
        pkb[0:DEC_T, :] = knb_ref[...]
        pvb[0:DEC_T, :] = vnb_ref[...]
        visible = (lane <= t_row) & (lane < DEC_T)
        bias_a = jnp.where(visible, -slope * (t_row - lane).astype(F32), NEG_INF)
        _online_update(new_logits(qa_ref, pka) + bias_a, head_slab(pva), ma, la, acca)
        bias_b = jnp.where(visible, -_expand_heads(cnew_ref[...]), NEG_INF)
        _online_update(new_logits(qb_ref, pkb) * fox_scale + bias_b, head_slab(pvb), mb, lb, accb)

        lam = _lambda_value(lq1, lk1, lq2, lk2)
        na = acca[...] / la[...]
        o = na - lam * pltpu.roll(na, ROWS - DEC_T, axis=0)
        o = o * lax.rsqrt(jnp.mean(o * o, axis=-1, keepdims=True) + RMS_EPS) * g_ref[...]
        o = o * (1.0 - LAM_INIT)
        nb = accb[...] / lb[...]
        for h in range(N_HEADS):
            oa_ref[:, h * HEAD_W:(h + 1) * HEAD_W] = o[h * SUBLANES:h * SUBLANES + DEC_T, :]
            ob_ref[:, h * HEAD_W:(h + 1) * HEAD_W] = nb[h * SUBLANES:h * SUBLANES + DEC_T, :]


def _decode_attention(page_table, lams, subln_g, qa8, qb8, pools, suffix, new_kv, cnew):
    n_seq = page_table.shape[0]

    def page_of(b, s, pt, g):
        return pt[b, N_PAGES - 1 - (jnp.minimum(s, N_STEPS - 1) * PPS + g)]

    def page_specs(shape):
        return [pl.BlockSpec((None,) + shape, lambda b, s, pt, g=g: (page_of(b, s, pt, g), 0, 0))
                for g in range(PPS)]

    def head_group_specs():
        return [pl.BlockSpec((None, PAGE, None, SUBLANES, HEAD_W),
                             lambda b, s, pt, g=g, hg=hg: (page_of(b, s, pt, g), 0, hg, 0, 0))
                for g in range(PPS) for hg in range(HEAD_GROUPS)]

    def seq_map(b, s, pt):
        return (b, 0, 0)

    def const_map(b, s, pt):
        return (0, 0)

    in_specs = ([pl.BlockSpec((1, DK_A), const_map)] * 4 + [pl.BlockSpec((1, HEAD_W), const_map)]
                + [pl.BlockSpec((None, ROWS, HEAD_W), seq_map)] * 2
                + page_specs((WIDTH, PAGE))
                + head_group_specs() * 3
                + page_specs((N_HEADS, PAGE))
                + [pl.BlockSpec((None, DEC_T, WIDTH), seq_map)] * 4
                + [pl.BlockSpec((None, N_HEADS, LANES), seq_map)])
    out_spec = pl.BlockSpec((None, DEC_T, WIDTH), seq_map)
    scratch = ([pltpu.VMEM((ROWS, 1), F32), pltpu.VMEM((ROWS, 1), F32), pltpu.VMEM((ROWS, HEAD_W), F32)] * 2
               + [pltpu.VMEM((ROWS, LANES), F32)]
               + [pltpu.VMEM((PAGE, WIDTH), F32)] * 4)
    block_bytes = 4 * PPS * _nbytes((PAGE, WIDTH), F32) + 12 * _nbytes((ROWS, LANES), F32)
    scratch_bytes = 4 * _nbytes((PAGE, WIDTH), F32) + 8 * _nbytes((ROWS, LANES), F32)
    page_args = ([pools[0]] * PPS + [p for pool in pools[1:] for p in [pool] * (PPS * HEAD_GROUPS)]
                 + [suffix] * PPS)
    return pl.pallas_call(
        _decode_kernel,
        name="paged_decode_attention",
        grid_spec=pltpu.PrefetchScalarGridSpec(
            num_scalar_prefetch=1,
            grid=(n_seq, N_STEPS + 1),
            in_specs=in_specs,
            out_specs=[out_spec, out_spec],
            scratch_shapes=scratch),
        out_shape=[jax.ShapeDtypeStruct((n_seq, DEC_T, WIDTH), F32)] * 2,
        compiler_params=pltpu.CompilerParams(
            dimension_semantics=("arbitrary", "arbitrary"),
            vmem_limit_bytes=_vmem_limit(block_bytes, scratch_bytes,
                                         256 * _nbytes((ROWS, LANES), F32))),
    )(page_table, *lams, subln_g.reshape(1, HEAD_W), qa8, qb8, *page_args, *new_kv, cnew)


TN_FF = 256


def _conv_gate(a, r1, r2, b, cw, cb):
    conv = cb + (cw[0:1] * r2 + cw[1:2] * r1 + cw[2:3] * a)
    return jax.nn.silu(conv) * b


def _cast_weights(wa_ref, wb_ref, wabf, wbbf):
    @pl.when(pl.program_id(1) == 0)
    def _():
        wabf[...] = wa_ref[...].astype(BF16)
        wbbf[...] = wb_ref[...].astype(BF16)


def _ffn_up_prompt_kernel(h_ref, wa_ref, wb_ref, cw_ref, cb_ref, act_ref, tail_ref,
                          wabf, wbbf, prev_ref, *, tiles_per_seq):
    _cast_weights(wa_ref, wb_ref, wabf, wbbf)
    t = pl.program_id(1) % tiles_per_seq
    hv = h_ref[...]
    a = _dot(hv, wabf[...])
    b = _dot(hv, wbbf[...])
    tm = a.shape[0]
    prev = jnp.where(t == 0, 0.0, prev_ref[...])
    row8 = lax.broadcasted_iota(jnp.int32, (SUBLANES, a.shape[1]), 0)
    r1 = pltpu.roll(a, 1, axis=0)
    r2 = pltpu.roll(a, 2, axis=0)
    r1_first = jnp.where(row8 < 1, pltpu.roll(prev, 1, axis=0), r1[0:SUBLANES])
    r2_first = jnp.where(row8 < 2, pltpu.roll(prev, 2, axis=0), r2[0:SUBLANES])
    r1 = jnp.concatenate([r1_first, r1[SUBLANES:tm]], axis=0)
    r2 = jnp.concatenate([r2_first, r2[SUBLANES:tm]], axis=0)
    act_ref[...] = _conv_gate(a, r1, r2, b, cw_ref[...], cb_ref[...]).astype(act_ref.dtype)
    last = a[tm - SUBLANES:tm]
    prev_ref[...] = last
    tail_ref[...] = last


def _ffn_up_sample_kernel(h_ref, wa_ref, wb_ref, cw_ref, cb_ref, s0_ref, s1_ref, act_ref, a_ref,
                          wabf, wbbf):
    _cast_weights(wa_ref, wb_ref, wabf, wbbf)
    hv = h_ref[...]
    a = _dot(hv, wabf[...])
    b = _dot(hv, wbbf[...])
    t = lax.broadcasted_iota(jnp.int32, a.shape, 0) % DEC_T
    s0, s1 = s0_ref[...], s1_ref[...]
    r1 = jnp.where(t == 0, s1, pltpu.roll(a, 1, axis=0))
    r2 = jnp.where(t == 0, s0, jnp.where(t == 1, s1, pltpu.roll(a, 2, axis=0)))
    act_ref[...] = _conv_gate(a, r1, r2, b, cw_ref[...], cb_ref[...]).astype(act_ref.dtype)
    a_ref[...] = a


def _ffn_up_specs(tm, k, nj):
    tn = TN_FF
    return [pl.BlockSpec((tm, k), lambda j, i: (i, 0)),
            pl.BlockSpec((k, tn), lambda j, i: (0, j)),
            pl.BlockSpec((k, tn), lambda j, i: (0, j + nj)),
            pl.BlockSpec((3, tn), lambda j, i: (0, j)),
            pl.BlockSpec((1, tn), lambda j, i: (0, j))]


def _ffn_up_prompt(h_bf, w_up, conv_w, conv_b, *, tm):
    m, k = h_bf.shape
    tn = TN_FF
    nj, ni, tps = D_FF // tn, m // tm, SEQ // tm
    kern = functools.partial(_ffn_up_prompt_kernel, tiles_per_seq=tps)
    block_bytes = (_nbytes((tm, k), BF16) + 2 * _nbytes((k, tn), F32) + _nbytes((tm, tn), BF16)
                   + 5 * _nbytes((SUBLANES, tn), F32))
    scratch_bytes = 2 * _nbytes((k, tn), BF16) + _nbytes((SUBLANES, tn), F32)
    return pl.pallas_call(
        kern,
        name="ffn_up_conv_prompt",
        grid=(nj, ni),
        in_specs=_ffn_up_specs(tm, k, nj),
        out_specs=[pl.BlockSpec((tm, tn), lambda j, i: (i, j)),
                   pl.BlockSpec((SUBLANES, tn), lambda j, i: (i // tps, j))],
        out_shape=[jax.ShapeDtypeStruct((m, D_FF), BF16),
                   jax.ShapeDtypeStruct((m // SEQ * SUBLANES, D_FF), F32)],
        scratch_shapes=[pltpu.VMEM((k, tn), BF16), pltpu.VMEM((k, tn), BF16),
                        pltpu.VMEM((SUBLANES, tn), F32)],
        compiler_params=pltpu.CompilerParams(
            dimension_semantics=("arbitrary", "arbitrary"),
            vmem_limit_bytes=_vmem_limit(block_bytes, scratch_bytes, 8 * _nbytes((tm, tn), F32))),
    )(h_bf, w_up, w_up, conv_w, conv_b.reshape(1, D_FF))


def _ffn_up_sample(h_bf, w_up, conv_w, conv_b, state):
    m, k = h_bf.shape
    tn = TN_FF
    nj = D_FF // tn
    s0 = jnp.repeat(state[:, 0], DEC_T, axis=0)
    s1 = jnp.repeat(state[:, 1], DEC_T, axis=0)
    block_bytes = (_nbytes((m, k), BF16) + 2 * _nbytes((k, tn), F32) + _nbytes((m, tn), BF16)
                   + 3 * _nbytes((m, tn), F32))
    scratch_bytes = 2 * _nbytes((k, tn), BF16)
    tile = pl.BlockSpec((m, tn), lambda j, i: (0, j))
    return pl.pallas_call(
        _ffn_up_sample_kernel,
        name="ffn_up_conv_sample",
        grid=(nj, 1),
        in_specs=_ffn_up_specs(m, k, nj) + [tile, tile],
        out_specs=[tile, tile],
        out_shape=[jax.ShapeDtypeStruct((m, D_FF), BF16), jax.ShapeDtypeStruct((m, D_FF), F32)],
        scratch_shapes=[pltpu.VMEM((k, tn), BF16), pltpu.VMEM((k, tn), BF16)],
        compiler_params=pltpu.CompilerParams(
            dimension_semantics=("arbitrary", "arbitrary"),
            vmem_limit_bytes=_vmem_limit(block_bytes, scratch_bytes, 8 * _nbytes((m, tn), F32))),
    )(h_bf, w_up, w_up, conv_w, conv_b.reshape(1, D_FF), s0, s1)


TN = 512


def _trunk(tag, x_f32, attn_a, attn_b, gates, w, *, tm, ffn_up_fn):
    gated = _merge("merge_" + tag, attn_a, attn_b, w["w_proj_a"], w["w_proj_b"], gates, tm=tm, tn=TN)
    (s1,) = _matmul("out_proj_" + tag, gated, w["w_out"], col0=0, ncols=D_MODEL, tm=tm, tn=TN,
                    out_dtypes=(F32,), epilogue=_ep_residual, extras=(x_f32,), extra_col0=(0,))
    h_f32, h_bf = _layer_norm("ln1_" + tag, s1, w["ln1_g"], w["ln1_b"], tm=256)
    act, a_info = ffn_up_fn(h_bf)
    (s2,) = _matmul("ffn_down_" + tag, act, w["w_ffn_down_bf"], col0=0, ncols=D_MODEL,
                    tm=min(tm, 512), tn=TN, out_dtypes=(F32,), epilogue=_ep_residual,
                    extras=(h_f32,), extra_col0=(0,))
    y, _ = _layer_norm("ln2_" + tag, s2, w["ln2_g"], w["ln2_b"], tm=256)
    return y, a_info


def kernel(x_prompt, x_sample, cache_diff_k, cache_diff_v, cache_fox_k, cache_fox_v, cache_fox_logf,
           state_ffn_conv, page_table, w_in, b_f, lambda_q1, lambda_k1, lambda_q2, lambda_k2, subln_g,
           w_proj_a, w_proj_b, w_out, ln1_g, ln1_b, w_ffn_up, conv_w, conv_b, w_ffn_down, ln2_g, ln2_b):
    n_batch, seq, d_model = x_prompt.shape
    n_seq, dec_t, _ = x_sample.shape
    assert (seq, d_model, dec_t) == (SEQ, D_MODEL, DEC_T) and w_in.shape[0] == DEPTH == 1
    mp, ms = n_batch * seq, n_seq * dec_t

    w_in_t = jnp.transpose(w_in[0], (1, 0))
    off_f = 6 * WIDTH
    off_g = off_f + N_HEADS
    w_f_t = w_in_t[off_f:off_g]
    w_gate_t = w_in_t[off_g:]
    w = dict(w_proj_a=w_proj_a[0], w_proj_b=w_proj_b[0], w_out=w_out[0], ln1_g=ln1_g[0], ln1_b=ln1_b[0],
             w_ffn_down_bf=w_ffn_down[0].astype(BF16), ln2_g=ln2_g[0], ln2_b=ln2_b[0])
    w_up = w_ffn_up[0]
    lams = tuple(v.reshape(1, DK_A) for v in (lambda_q1[0], lambda_k1[0], lambda_q2[0], lambda_k2[0]))

    xp = x_prompt.reshape(mp, d_model)
    xs = x_sample.reshape(ms, d_model)
    xp_bf, xs_bf = xp.astype(BF16), xs.astype(BF16)

    def project(tag, x_bf, tm, with_bf_copy):
        res = {}
        names = ("qa", "ka", "va", "qb", "kb", "vb")
        for idx, name in enumerate(names):
            common = dict(col0=idx * WIDTH, ncols=WIDTH, tm=tm, tn=TN, w_t=True)
            call = "proj_%s_%s" % (name, tag)
            if name[0] == "q":
                (res[name],) = _matmul(call, x_bf, w_in_t, out_dtypes=(BF16,), epilogue=_ep_identity, **common)
            elif with_bf_copy:
                res[name], res[name + "_bf"] = _matmul(call, x_bf, w_in_t, out_dtypes=(F32, BF16),
                                                       epilogue=_ep_dup, **common)
            else:
                (res[name],) = _matmul(call, x_bf, w_in_t, out_dtypes=(F32,), epilogue=_ep_identity, **common)
        (res["gates"],) = _matmul("proj_gates_" + tag, x_bf, w_gate_t, col0=0, ncols=2 * D_MODEL, tm=tm,
                                  tn=TN, out_dtypes=(F32,), epilogue=_ep_sigmoid, w_t=True)
        return res

    pp = project("prompt", xp_bf, 1024, True)
    ps = project("sample", xs_bf, ms, False)

    tmf = 512
    tri = (jnp.arange(tmf)[:, None] <= jnp.arange(tmf)[None, :])
    logf_p, csum_p = _fgate("fgate_prompt", xp_bf, w_f_t, b_f[0], tri.astype(BF16), n_seq=n_batch,
                            tiles_per_seq=seq // tmf, tm=tmf, carry_tiles=True)
    grp = jnp.arange(ms) // dec_t
    blockdiag = tri[:ms, :ms] & (grp[:, None] == grp[None, :])
    logf_s, csum_s = _fgate("fgate_sample", xs_bf, w_f_t, b_f[0], blockdiag.astype(BF16), n_seq=1,
                            tiles_per_seq=1, tm=ms, carry_tiles=False)

    oa_p = _diff_prompt(pp["qa"], pp["ka_bf"], pp["va_bf"], lams, subln_g[0], n_batch)
    csum_rows = csum_p.reshape(N_HEADS, n_batch, 1, seq).transpose(1, 0, 2, 3)
    ob_p = _fox_prompt(pp["qb"], pp["kb_bf"], pp["vb_bf"], csum_rows, n_batch)

    n_pool = cache_diff_k.shape[1]
    kt_a = jnp.transpose(cache_diff_k[0], (0, 2, 3, 4, 1)).reshape(n_pool, WIDTH, PAGE)
    grouped = (n_pool, PAGE, HEAD_GROUPS, SUBLANES, HEAD_W)
    pools = (kt_a, cache_diff_v[0].reshape(grouped), cache_fox_k[0].reshape(grouped),
             cache_fox_v[0].reshape(grouped))
    suffix = _page_suffix_sums(cache_fox_logf[0])
    qa = ps["qa"].astype(F32).reshape(n_seq, dec_t, N_HEADS, 2, DK_A) * (DK_A ** -0.5)
    qa = qa.transpose(0, 2, 3, 1, 4)
    eye = jnp.eye(2, dtype=F32)[None, None, :, None, :, None]
    qa8 = (qa[:, :, :, :, None, :] * eye).reshape(n_seq, ROWS, HEAD_W)
    qb = ps["qb"].astype(F32).reshape(n_seq, dec_t, N_HEADS, HEAD_W).transpose(0, 2, 1, 3)
    qb8 = jnp.concatenate([qb, jnp.zeros_like(qb)], axis=2).reshape(n_seq, ROWS, HEAD_W)
    new_kv = tuple(ps[nm].reshape(n_seq, dec_t, WIDTH) for nm in ("ka", "va", "kb", "vb"))
    cnew = csum_s.reshape(N_HEADS, n_seq, dec_t).transpose(1, 0, 2)
    cnew = jnp.zeros((n_seq, N_HEADS, LANES), F32).at[:, :, :dec_t].set(cnew)
    oa_s, ob_s = _decode_attention(page_table, lams, subln_g[0], qa8, qb8, pools, suffix, new_kv, cnew)
    oa_s = oa_s.reshape(ms, WIDTH).astype(BF16)
    ob_s = ob_s.reshape(ms, WIDTH).astype(BF16)

    conv_w0, conv_b0 = conv_w[0], conv_b[0]
    y_p, tail_p = _trunk("prompt", xp, oa_p, ob_p, pp["gates"], w, tm=1024,
                         ffn_up_fn=lambda h: _ffn_up_prompt(h, w_up, conv_w0, conv_b0, tm=1024))
    y_s, a_s = _trunk("sample", xs, oa_s, ob_s, ps["gates"], w, tm=ms,
                      ffn_up_fn=lambda h: _ffn_up_sample(h, w_up, conv_w0, conv_b0, state_ffn_conv[0]))

    def heads(x, n, t, *tail):
        return x.reshape(1, n, t, N_HEADS, *tail)

    conv_p = tail_p.reshape(n_batch, SUBLANES, D_FF)[None, :, SUBLANES - 2:, :]
    conv_s = a_s.reshape(n_seq, dec_t, D_FF)[None, :, dec_t - 2:, :]
    return (y_p.reshape(n_batch, seq, d_model), y_s.reshape(n_seq, dec_t, d_model),
            heads(pp["ka"], n_batch, seq, 2, DK_A), heads(pp["va"], n_batch, seq, HEAD_W),
            heads(pp["kb"], n_batch, seq, HEAD_W), heads(pp["vb"], n_batch, seq, HEAD_W),
            heads(logf_p, n_batch, seq), conv_p,
            heads(ps["ka"], n_seq, dec_t, 2, DK_A), heads(ps["va"], n_seq, dec_t, HEAD_W),
            heads(ps["kb"], n_seq, dec_t, HEAD_W), heads(ps["vb"], n_seq, dec_t, HEAD_W),
            heads(logf_s, n_seq, dec_t), conv_s)
```

```python
import functools
import math

import jax
import jax.numpy as jnp
from jax import lax
from jax.experimental import pallas as pl
from jax.experimental.pallas import tpu as pltpu

F32 = jnp.float32
BF16 = jnp.bfloat16

V7X_VMEM_CAP = 60000 * 1024
LANES = 128
SUBLANES = 8

D_MODEL = 4096
SEQ = 2048
PAGE = 128
N_HEADS = 16
HEAD_W = 128
DK_A = 64
WIDTH = N_HEADS * HEAD_W
D_FF = 11008
DEPTH = 1
ALPHA = (2.0 * DEPTH) ** 0.25
LN_EPS = 1e-5
RMS_EPS = 1e-5
LAM_INIT = 0.8 - 0.6 * math.exp(-0.3 * 0)
NEG_INF = float("-inf")


def _vmem_limit(block_bytes, scratch_bytes=0, temp_bytes=0):
    need = 2 * block_bytes + scratch_bytes + temp_bytes + (2 << 20)
    return int(min(max(need, 16 << 20), V7X_VMEM_CAP))


def _nbytes(shape, dtype):
    return math.prod(shape) * jnp.dtype(dtype).itemsize


def _dot(a, b):
    return jnp.dot(a, b, preferred_element_type=F32)


def _dot_nt(a, b):
    return lax.dot_general(a, b, (((1,), (1,)), ((), ())), preferred_element_type=F32)


def _mm_kernel(*refs, n_extra, n_out, epilogue, cast_w, w_t):
    a_ref, w_ref = refs[0], refs[1]
    extra = refs[2:2 + n_extra]
    outs = refs[2 + n_extra:2 + n_extra + n_out]
    if cast_w:
        wbf_ref = refs[2 + n_extra + n_out]

        @pl.when(pl.program_id(1) == 0)
        def _():
            wbf_ref[...] = w_ref[...].astype(BF16)

        w = wbf_ref[...]
    else:
        w = w_ref[...]
    acc = _dot_nt(a_ref[...], w) if w_t else _dot(a_ref[...], w)
    vals = epilogue(acc, *[e[...] for e in extra])
    for o, v in zip(outs, vals):
        o[...] = v.astype(o.dtype)


def _matmul(name, a, w, *, col0, ncols, tm, tn, out_dtypes, epilogue, extras=(), extra_col0=(), w_t=False):
    m, k = a.shape
    assert m % tm == 0 and ncols % tn == 0 and col0 % tn == 0
    nj, ni = ncols // tn, m // tm
    jb = col0 // tn
    cast_w = w.dtype != BF16
    w_block = (tn, k) if w_t else (k, tn)
    w_map = (lambda j, i: (j + jb, 0)) if w_t else (lambda j, i: (0, j + jb))
    in_specs = [pl.BlockSpec((tm, k), lambda j, i: (i, 0)), pl.BlockSpec(w_block, w_map)]
    block_bytes = _nbytes((tm, k), a.dtype) + _nbytes(w_block, w.dtype)
    for e, c0 in zip(extras, extra_col0):
        assert c0 % tn == 0
        eb = c0 // tn
        in_specs.append(pl.BlockSpec((tm, tn), lambda j, i, eb=eb: (i, j + eb)))
        block_bytes += _nbytes((tm, tn), e.dtype)
    out_shape = [jax.ShapeDtypeStruct((m, ncols), d) for d in out_dtypes]
    out_specs = [pl.BlockSpec((tm, tn), lambda j, i: (i, j)) for _ in out_dtypes]
    for d in out_dtypes:
        block_bytes += _nbytes((tm, tn), d)
    scratch = [pltpu.VMEM(w_block, BF16)] if cast_w else []
    scratch_bytes = _nbytes(w_block, BF16) if cast_w else 0
    kern = functools.partial(_mm_kernel, n_extra=len(extras), n_out=len(out_dtypes),
                             epilogue=epilogue, cast_w=cast_w, w_t=w_t)
    return pl.pallas_call(
        kern,
        name=name,
        grid=(nj, ni),
        in_specs=in_specs,
        out_specs=out_specs,
        out_shape=out_shape,
        scratch_shapes=scratch,
        compiler_params=pltpu.CompilerParams(
            dimension_semantics=("arbitrary", "arbitrary"),
            vmem_limit_bytes=_vmem_limit(block_bytes, scratch_bytes, 2 * _nbytes((tm, tn), F32))),
    )(a, w, *extras)


def _ep_identity(acc):
    return (acc,)


def _ep_dup(acc):
    return (acc, acc)


def _ep_sigmoid(acc):
    return (jax.nn.sigmoid(acc),)


def _ep_residual(acc, res):
    return (ALPHA * res + acc,)


def _merge_kernel(oa_ref, ob_ref, wa_ref, wb_ref, ga_ref, gb_ref, out_ref, wabf, wbbf):
    @pl.when(pl.program_id(1) == 0)
    def _():
        wabf[...] = wa_ref[...].astype(BF16)
        wbbf[...] = wb_ref[...].astype(BF16)

    ba = _dot(oa_ref[...], wabf[...])
    bb = _dot(ob_ref[...], wbbf[...])
    out_ref[...] = (ga_ref[...] * ba + gb_ref[...] * bb).astype(out_ref.dtype)


def _merge(name, oa, ob, wa, wb, gates, *, tm, tn):
    m, k = oa.shape
    n = wa.shape[1]
    nj, ni = n // tn, m // tm
    gb0 = n // tn
    block_bytes = (2 * _nbytes((tm, k), BF16) + 2 * _nbytes((k, tn), F32)
                   + 2 * _nbytes((tm, tn), F32) + _nbytes((tm, tn), BF16))
    return pl.pallas_call(
        _merge_kernel,
        name=name,
        grid=(nj, ni),
        in_specs=[pl.BlockSpec((tm, k), lambda j, i: (i, 0)),
                  pl.BlockSpec((tm, k), lambda j, i: (i, 0)),
                  pl.BlockSpec((k, tn), lambda j, i: (0, j)),
                  pl.BlockSpec((k, tn), lambda j, i: (0, j)),
                  pl.BlockSpec((tm, tn), lambda j, i: (i, j)),
                  pl.BlockSpec((tm, tn), lambda j, i: (i, j + gb0))],
        out_specs=pl.BlockSpec((tm, tn), lambda j, i: (i, j)),
        out_shape=jax.ShapeDtypeStruct((m, n), BF16),
        scratch_shapes=[pltpu.VMEM((k, tn), BF16), pltpu.VMEM((k, tn), BF16)],
        compiler_params=pltpu.CompilerParams(
            dimension_semantics=("arbitrary", "arbitrary"),
            vmem_limit_bytes=_vmem_limit(block_bytes, 2 * _nbytes((k, tn), BF16),
                                         4 * _nbytes((tm, tn), F32))),
    )(oa, ob, wa, wb, gates, gates)


def _ln_kernel(x_ref, g_ref, b_ref, of_ref, ob_ref):
    x = x_ref[...]
    mu = jnp.mean(x, axis=-1, keepdims=True)
    xc = x - mu
    var = jnp.mean(xc * xc, axis=-1, keepdims=True)
    y = xc * lax.rsqrt(var + LN_EPS) * g_ref[...] + b_ref[...]
    of_ref[...] = y
    ob_ref[...] = y.astype(BF16)


def _layer_norm(name, x, g, b, *, tm):
    m, d = x.shape
    block_bytes = _nbytes((tm, d), F32) * 2 + _nbytes((tm, d), BF16)
    return pl.pallas_call(
        _ln_kernel,
        name=name,
        grid=(m // tm,),
        in_specs=[pl.BlockSpec((tm, d), lambda i: (i, 0)),
                  pl.BlockSpec((1, d), lambda i: (0, 0)),
                  pl.BlockSpec((1, d), lambda i: (0, 0))],
        out_specs=[pl.BlockSpec((tm, d), lambda i: (i, 0)),
                   pl.BlockSpec((tm, d), lambda i: (i, 0))],
        out_shape=[jax.ShapeDtypeStruct((m, d), F32), jax.ShapeDtypeStruct((m, d), BF16)],
        compiler_params=pltpu.CompilerParams(
            dimension_semantics=("arbitrary",),
            vmem_limit_bytes=_vmem_limit(block_bytes, 0, 3 * _nbytes((tm, d), F32))),
    )(x, g.reshape(1, d), b.reshape(1, d))


def _log_sigmoid(x):
    return jnp.minimum(x, 0.0) - jnp.log1p(jnp.exp(-jnp.abs(x)))


def _split3(x):
    hi = x.astype(BF16)
    r1 = x - hi.astype(F32)
    mid = r1.astype(BF16)
    lo = (r1 - mid.astype(F32)).astype(BF16)
    return hi, mid, lo


def _dot_exact01(x, ones_bf16):
    hi, mid, lo = _split3(x)
    return _dot(hi, ones_bf16) + _dot(mid, ones_bf16) + _dot(lo, ones_bf16)


def _fgate_kernel(x_ref, wf_ref, wft_ref, bf_row_ref, bf_col_ref, u_ref,
                  logf_ref, csum_ref, carry_ref, *, carry_tiles):
    x = x_ref[...]
    f_nat = _dot(x, wf_ref[...])
    logf_ref[...] = _log_sigmoid(f_nat[:, :N_HEADS] + bf_row_ref[...])
    f_t = _dot_nt(wft_ref[...], x)
    logf_t = _log_sigmoid(f_t + bf_col_ref[...])
    c = _dot_exact01(logf_t, u_ref[...])
    if carry_tiles:
        t = pl.program_id(1)

        @pl.when(t == 0)
        def _():
            carry_ref[...] = jnp.zeros_like(carry_ref)

        c = c + carry_ref[:, 0:1]
        carry_ref[...] = jnp.broadcast_to(c[:, -1:], carry_ref.shape)
    csum_ref[...] = c


def _fgate(name, x_bf, w_f_t, b_f, u_mat, *, n_seq, tiles_per_seq, tm, carry_tiles):
    m, k = x_bf.shape
    wft = w_f_t.astype(BF16)
    wf_pad = jnp.zeros((k, LANES), BF16).at[:, :N_HEADS].set(wft.T)
    bf_row = b_f.reshape(1, N_HEADS)
    bf_col = b_f.reshape(N_HEADS, 1)
    kern = functools.partial(_fgate_kernel, carry_tiles=carry_tiles)
    tps = tiles_per_seq
    block_bytes = (_nbytes((tm, k), BF16) + _nbytes((k, LANES), BF16) + _nbytes((N_HEADS, k), BF16)
                   + _nbytes((tm, tm), BF16) + _nbytes((tm, LANES), F32) + _nbytes((N_HEADS, tm), F32))
    return pl.pallas_call(
        kern,
        name=name,
        grid=(n_seq, tps),
        in_specs=[pl.BlockSpec((tm, k), lambda n, t: (n * tps + t, 0)),
                  pl.BlockSpec((k, LANES), lambda n, t: (0, 0)),
                  pl.BlockSpec((N_HEADS, k), lambda n, t: (0, 0)),
                  pl.BlockSpec((1, N_HEADS), lambda n, t: (0, 0)),
                  pl.BlockSpec((N_HEADS, 1), lambda n, t: (0, 0)),
                  pl.BlockSpec((tm, tm), lambda n, t: (0, 0))],
        out_specs=[pl.BlockSpec((tm, N_HEADS), lambda n, t: (n * tps + t, 0)),
                   pl.BlockSpec((N_HEADS, tm), lambda n, t: (0, n * tps + t))],
        out_shape=[jax.ShapeDtypeStruct((m, N_HEADS), F32),
                   jax.ShapeDtypeStruct((N_HEADS, m), F32)],
        scratch_shapes=[pltpu.VMEM((N_HEADS, LANES), F32)],
        compiler_params=pltpu.CompilerParams(
            dimension_semantics=("arbitrary", "arbitrary"),
            vmem_limit_bytes=_vmem_limit(block_bytes, 0, 8 * _nbytes((tm, LANES), F32))),
    )(x_bf, wf_pad, wft, bf_row, bf_col, u_mat)


def _suffix_kernel(x_ref, m_ref, o_ref):
    o_ref[...] = _dot_exact01(x_ref[...], m_ref[...])


def _page_suffix_sums(logf_pool):
    n_pool = logf_pool.shape[0]
    rows = n_pool * N_HEADS
    x = jnp.transpose(logf_pool, (0, 2, 1)).reshape(rows, PAGE)
    sel = (jnp.arange(PAGE)[:, None] >= jnp.arange(PAGE)[None, :]).astype(BF16)
    tr = 4096
    assert rows % tr == 0
    block_bytes = 2 * _nbytes((tr, PAGE), F32) + _nbytes((PAGE, PAGE), BF16)
    out = pl.pallas_call(
        _suffix_kernel,
        name="page_suffix_sums",
        grid=(rows // tr,),
        in_specs=[pl.BlockSpec((tr, PAGE), lambda i: (i, 0)),
                  pl.BlockSpec((PAGE, PAGE), lambda i: (0, 0))],
        out_specs=pl.BlockSpec((tr, PAGE), lambda i: (i, 0)),
        out_shape=jax.ShapeDtypeStruct((rows, PAGE), F32),
        compiler_params=pltpu.CompilerParams(
            dimension_semantics=("arbitrary",),
            vmem_limit_bytes=_vmem_limit(block_bytes, 0, 6 * _nbytes((tr, PAGE), F32))),
    )(x, sel)
    return out.reshape(n_pool, N_HEADS, PAGE)


TQ = 256
NQB = SEQ // TQ


def _lambda_value(lq1, lk1, lq2, lk2):
    s1 = jnp.sum(lq1[...] * lk1[...], axis=-1, keepdims=True)
    s2 = jnp.sum(lq2[...] * lk2[...], axis=-1, keepdims=True)
    return jnp.exp(s1) - jnp.exp(s2) + LAM_INIT


def _causal_strip():
    r = lax.broadcasted_iota(jnp.int32, (TQ, SEQ), 0)
    j = lax.broadcasted_iota(jnp.int32, (TQ, SEQ), 1)
    return r - j + (SEQ - TQ)


def _diff_prompt_kernel(lq1, lk1, lq2, lk2, g_ref, q_ref, k_ref, v_ref, o_ref, bias_ref):
    h = pl.program_id(1)
    lam = _lambda_value(lq1, lk1, lq2, lk2)
    slope = jnp.exp2(jnp.full((1, 1), -0.5, F32) * (h + 1).astype(F32))
    d = _causal_strip()
    bias_ref[...] = jnp.where(d >= 0, -slope * d.astype(F32), NEG_INF)
    lane = lax.broadcasted_iota(jnp.int32, (TQ, HEAD_W), 1)
    g = g_ref[...]
    for b in range(NQB):
        s_len = (b + 1) * TQ
        q = q_ref[b * TQ:(b + 1) * TQ, :] * jnp.asarray(DK_A ** -0.5, BF16)
        q1 = jnp.where(lane < DK_A, q, jnp.zeros_like(q))
        q2 = jnp.where(lane >= DK_A, q, jnp.zeros_like(q))
        k = k_ref[0:s_len, :]
        v = v_ref[0:s_len, :]
        c0 = (NQB - 1 - b) * TQ
        bias = bias_ref[:, c0:c0 + s_len]
        outs = []
        for qm in (q1, q2):
            s = _dot_nt(qm, k) + bias
            m = jnp.max(s, axis=-1, keepdims=True)
            e = jnp.exp(s - m)
            l = jnp.sum(e, axis=-1, keepdims=True)
            outs.append(_dot(e.astype(BF16), v) / l)
        o = outs[0] - lam * outs[1]
        o = o * lax.rsqrt(jnp.mean(o * o, axis=-1, keepdims=True) + RMS_EPS) * g
        o_ref[b * TQ:(b + 1) * TQ, :] = (o * (1.0 - LAM_INIT)).astype(o_ref.dtype)


def _fox_prompt_kernel(q_ref, k_ref, v_ref, c_ref, o_ref, mask_ref):
    d = _causal_strip()
    mask_ref[...] = jnp.where(d >= 0, 0.0, NEG_INF).astype(F32)
    scale = HEAD_W ** -0.5
    for b in range(NQB):
        s_len = (b + 1) * TQ
        q = q_ref[b * TQ:(b + 1) * TQ, :]
        k = k_ref[0:s_len, :]
        v = v_ref[0:s_len, :]
        c0 = (NQB - 1 - b) * TQ
        bias = mask_ref[:, c0:c0 + s_len] - c_ref[:, 0:s_len]
        s = _dot_nt(q, k) * scale + bias
        m = jnp.max(s, axis=-1, keepdims=True)
        e = jnp.exp(s - m)
        l = jnp.sum(e, axis=-1, keepdims=True)
        o_ref[b * TQ:(b + 1) * TQ, :] = (_dot(e.astype(BF16), v) / l).astype(o_ref.dtype)


def _head_spec():
    return pl.BlockSpec((SEQ, HEAD_W), lambda n, h: (n, h))


def _small_spec(shape):
    return pl.BlockSpec(shape, lambda n, h: (0,) * len(shape))


_ATT_TEMP = 6 * TQ * SEQ * 4


def _diff_prompt(q, k, v, lams, subln_g, n_batch):
    block_bytes = 4 * _nbytes((SEQ, HEAD_W), BF16)
    return pl.pallas_call(
        _diff_prompt_kernel,
        name="diff_prompt_attention",
        grid=(n_batch, N_HEADS),
        in_specs=[_small_spec((1, DK_A))] * 4 + [_small_spec((1, HEAD_W))] + [_head_spec()] * 3,
        out_specs=_head_spec(),
        out_shape=jax.ShapeDtypeStruct(q.shape, BF16),
        scratch_shapes=[pltpu.VMEM((TQ, SEQ), F32)],
        compiler_params=pltpu.CompilerParams(
            dimension_semantics=("arbitrary", "arbitrary"),
            vmem_limit_bytes=_vmem_limit(block_bytes, _nbytes((TQ, SEQ), F32), _ATT_TEMP)),
    )(*lams, subln_g.reshape(1, HEAD_W), q, k, v)


def _fox_prompt(q, k, v, csum, n_batch):
    block_bytes = 4 * _nbytes((SEQ, HEAD_W), BF16) + _nbytes((8, SEQ), F32)
    return pl.pallas_call(
        _fox_prompt_kernel,
        name="fox_prompt_attention",
        grid=(n_batch, N_HEADS),
        in_specs=[_head_spec()] * 3 + [pl.BlockSpec((None, None, 1, SEQ), lambda n, h: (n, h, 0, 0))],
        out_specs=_head_spec(),
        out_shape=jax.ShapeDtypeStruct(q.shape, BF16),
        scratch_shapes=[pltpu.VMEM((TQ, SEQ), F32)],
        compiler_params=pltpu.CompilerParams(
            dimension_semantics=("arbitrary", "arbitrary"),
            vmem_limit_bytes=_vmem_limit(block_bytes, _nbytes((TQ, SEQ), F32), _ATT_TEMP)),
    )(q, k, v, csum)


N_PAGES = SEQ // PAGE
PPS = 4
N_STEPS = N_PAGES // PPS
DEC_T = 4
ROWS = N_HEADS * SUBLANES
HEAD_GROUPS = N_HEADS // SUBLANES


def _expand_heads(x):
    return jnp.concatenate(
        [jnp.broadcast_to(x[h:h + 1, :], (SUBLANES, x.shape[1])) for h in range(N_HEADS)], axis=0)


def _head_rows(x, h):
    return x[h * SUBLANES:(h + 1) * SUBLANES, :]


def _online_update(s, v_of_head, m_ref, l_ref, acc_ref):
    m_old = m_ref[...]
    m_new = jnp.maximum(m_old, jnp.max(s, axis=-1, keepdims=True))
    e = jnp.exp(s - m_new)
    alpha = jnp.exp(m_old - m_new)
    l_ref[...] = alpha * l_ref[...] + jnp.sum(e, axis=-1, keepdims=True)
    pv = jnp.concatenate([_dot(_head_rows(e, h), v_of_head(h)) for h in range(N_HEADS)], axis=0)
    acc_ref[...] = alpha * acc_ref[...] + pv
    m_ref[...] = m_new


def _decode_kernel(pt_ref, lq1, lk1, lq2, lk2, g_ref, qa_ref, qb_ref, *rest):
    del pt_ref
    n_blk = PPS * HEAD_GROUPS
    kta = rest[0:PPS]
    va, kb, vb = (rest[PPS + i * n_blk:PPS + (i + 1) * n_blk] for i in range(3))
    suf = rest[PPS + 3 * n_blk:2 * PPS + 3 * n_blk]
    (kna_ref, vna_ref, knb_ref, vnb_ref, cnew_ref, oa_ref, ob_ref,
     ma, la, acca, mb, lb, accb, carry, pka, pva, pkb, pvb) = rest[2 * PPS + 3 * n_blk:]
    b = pl.program_id(0)
    step = pl.program_id(1)
    row = lax.broadcasted_iota(jnp.int32, (ROWS, LANES), 0)
    lane = lax.broadcasted_iota(jnp.int32, (ROWS, LANES), 1)
    head_f = (row // SUBLANES + 1).astype(F32)
    t_row = row % DEC_T
    slope = jnp.exp2(-0.5 * head_f)
    fox_scale = HEAD_W ** -0.5

    @pl.when((b == 0) & (step == 0))
    def _():
        for p in (pka, pva, pkb, pvb):
            p[...] = jnp.zeros_like(p)

    @pl.when(step == 0)
    def _():
        for r in (ma, mb):
            r[...] = jnp.full_like(r, NEG_INF)
        for r in (la, lb, acca, accb, carry):
            r[...] = jnp.zeros_like(r)

    def strided_head(refs):
        def load(h):
            hg, h8 = divmod(h, SUBLANES)
            return jnp.concatenate(
                [refs[g * HEAD_GROUPS + hg].reshape(PAGE * SUBLANES, HEAD_W)[pl.ds(h8, PAGE, stride=SUBLANES), :]
                 for g in range(PPS)], axis=0)
        return load

    def cached_pages_step():
        q_pos = (SEQ + t_row).astype(F32)
        bias_a = []
        for g in range(PPS):
            page = N_PAGES - 1 - (step * PPS + g)
            key_pos = (page * PAGE).astype(F32) + lane.astype(F32)
            bias_a.append(-slope * (q_pos - key_pos))
        sa = jnp.concatenate(
            [_dot(qa_ref[h * SUBLANES:(h + 1) * SUBLANES, :],
                  jnp.concatenate([r[h * HEAD_W:(h + 1) * HEAD_W, :] for r in kta], axis=1))
             for h in range(N_HEADS)], axis=0)
        sa = sa + jnp.concatenate(bias_a, axis=1)
        _online_update(sa, strided_head(va), ma, la, acca)

        lane16 = lax.broadcasted_iota(jnp.int32, (N_HEADS, LANES), 1)
        c = carry[...]
        bias_b = []
        for g in range(PPS):
            sfx = suf[g][...]
            excl = jnp.where(lane16 == LANES - 1, 0.0, pltpu.roll(sfx, LANES - 1, axis=1))
            bias_b.append(_expand_heads(excl) + c)
            c = c + _expand_heads(jnp.broadcast_to(sfx[:, 0:1], (N_HEADS, LANES)))
        carry[...] = c
        k_of_head = strided_head(kb)
        sb = jnp.concatenate(
            [_dot_nt(qb_ref[h * SUBLANES:(h + 1) * SUBLANES, :], k_of_head(h)) for h in range(N_HEADS)],
            axis=0)
        sb = sb * fox_scale + jnp.concatenate(bias_b, axis=1)
        _online_update(sb, strided_head(vb), mb, lb, accb)

    def head_slab(ref):
        return lambda h: ref[:, h * HEAD_W:(h + 1) * HEAD_W]

    def new_logits(q_ref, k_ref):
        k_of_head = head_slab(k_ref)
        return jnp.concatenate(
            [_dot_nt(q_ref[h * SUBLANES:(h + 1) * SUBLANES, :], k_of_head(h)) for h in range(N_HEADS)],
            axis=0)

    cached_pages_step()

    @pl.when(step == N_STEPS - 1)
    def _():
        pka[0:DEC_T, :] = kna_ref[...]
        pva[0:DEC_T, :] = vna_ref[...]
        pkb[0:DEC_T, :] = knb_ref[...]
        pvb[0:DEC_T, :] = vnb_ref[...]
        visible = (lane <= t_row) & (lane < DEC_T)
        bias_a = jnp.where(visible, -slope * (t_row - lane).astype(F32), NEG_INF)
        _online_update(new_logits(qa_ref, pka) + bias_a, head_slab(pva), ma, la, acca)
        bias_b = jnp.where(visible, -_expand_heads(cnew_ref[...]), NEG_INF)
        _online_update(new_logits(qb_ref, pkb) * fox_scale + bias_b, head_slab(pvb), mb, lb, accb)

        lam = _lambda_value(lq1, lk1, lq2, lk2)
        na = acca[...] / la[...]
        o = na - lam * pltpu.roll(na, ROWS - DEC_T, axis=0)
        o = o * lax.rsqrt(jnp.mean(o * o, axis=-1, keepdims=True) + RMS_EPS) * g_ref[...]
        o = o * (1.0 - LAM_INIT)
        nb = accb[...] / lb[...]
        for h in range(N_HEADS):
            oa_ref[:, h * HEAD_W:(h + 1) * HEAD_W] = o[h * SUBLANES:h * SUBLANES + DEC_T, :]
            ob_ref[:, h * HEAD_W:(h + 1) * HEAD_W] = nb[h * SUBLANES:h * SUBLANES + DEC_T, :]


def _decode_attention(page_table, lams, subln_g, qa8, qb8, pools, suffix, new_kv, cnew):
    n_seq = page_table.shape[0]

    def page_of(b, s, pt, g):
        return pt[b, N_PAGES - 1 - (s * PPS + g)]

    def page_specs(shape):
        return [pl.BlockSpec((None,) + shape, lambda b, s, pt, g=g: (page_of(b, s, pt, g), 0, 0))
                for g in range(PPS)]

    def head_group_specs():
        return [pl.BlockSpec((None, PAGE, None, SUBLANES, HEAD_W),
                             lambda b, s, pt, g=g, hg=hg: (page_of(b, s, pt, g), 0, hg, 0, 0))
                for g in range(PPS) for hg in range(HEAD_GROUPS)]

    def seq_map(b, s, pt):
        return (b, 0, 0)

    def const_map(b, s, pt):
        return (0, 0)

    in_specs = ([pl.BlockSpec((1, DK_A), const_map)] * 4 + [pl.BlockSpec((1, HEAD_W), const_map)]
                + [pl.BlockSpec((None, ROWS, HEAD_W), seq_map)] * 2
                + page_specs((WIDTH, PAGE))
                + head_group_specs() * 3
                + page_specs((N_HEADS, PAGE))
                + [pl.BlockSpec((None, DEC_T, WIDTH), seq_map)] * 4
                + [pl.BlockSpec((None, N_HEADS, LANES), seq_map)])
    out_spec = pl.BlockSpec((None, DEC_T, WIDTH), seq_map)
    scratch = ([pltpu.VMEM((ROWS, 1), F32), pltpu.VMEM((ROWS, 1), F32), pltpu.VMEM((ROWS, HEAD_W), F32)] * 2
               + [pltpu.VMEM((ROWS, LANES), F32)]
               + [pltpu.VMEM((PAGE, WIDTH), F32)] * 4)
    block_bytes = 4 * PPS * _nbytes((PAGE, WIDTH), F32) + 12 * _nbytes((ROWS, LANES), F32)
    scratch_bytes = 4 * _nbytes((PAGE, WIDTH), F32) + 8 * _nbytes((ROWS, LANES), F32)
    page_args = ([pools[0]] * PPS + [p for pool in pools[1:] for p in [pool] * (PPS * HEAD_GROUPS)]
                 + [suffix] * PPS)
    return pl.pallas_call(
        _decode_kernel,
        name="paged_decode_attention",
        grid_spec=pltpu.PrefetchScalarGridSpec(
            num_scalar_prefetch=1,
            grid=(n_seq, N_STEPS),
            in_specs=in_specs,
            out_specs=[out_spec, out_spec],
            scratch_shapes=scratch),
        out_shape=[jax.ShapeDtypeStruct((n_seq, DEC_T, WIDTH), F32)] * 2,
        compiler_params=pltpu.CompilerParams(
            dimension_semantics=("arbitrary", "arbitrary"),
            vmem_limit_bytes=_vmem_limit(block_bytes, scratch_bytes,
                                         256 * _nbytes((ROWS, LANES), F32))),
    )(page_table, *lams, subln_g.reshape(1, HEAD_W), qa8, qb8, *page_args, *new_kv, cnew)


TN_FF = 256


def _conv_gate(a, r1, r2, b, cw, cb):
    conv = cb + (cw[0:1] * r2 + cw[1:2] * r1 + cw[2:3] * a)
    return jax.nn.silu(conv) * b


def _cast_weights(wa_ref, wb_ref, wabf, wbbf):
    @pl.when(pl.program_id(1) == 0)
    def _():
        wabf[...] = wa_ref[...].astype(BF16)
        wbbf[...] = wb_ref[...].astype(BF16)


def _ffn_up_prompt_kernel(h_ref, wa_ref, wb_ref, cw_ref, cb_ref, act_ref, tail_ref,
                          wabf, wbbf, prev_ref, *, tiles_per_seq):
    _cast_weights(wa_ref, wb_ref, wabf, wbbf)
    t = pl.program_id(1) % tiles_per_seq
    hv = h_ref[...]
    a = _dot(hv, wabf[...])
    b = _dot(hv, wbbf[...])
    tm = a.shape[0]
    prev = jnp.where(t == 0, 0.0, prev_ref[...])
    row8 = lax.broadcasted_iota(jnp.int32, (SUBLANES, a.shape[1]), 0)
    r1 = pltpu.roll(a, 1, axis=0)
    r2 = pltpu.roll(a, 2, axis=0)
    r1_first = jnp.where(row8 < 1, pltpu.roll(prev, 1, axis=0), r1[0:SUBLANES])
    r2_first = jnp.where(row8 < 2, pltpu.roll(prev, 2, axis=0), r2[0:SUBLANES])
    r1 = jnp.concatenate([r1_first, r1[SUBLANES:tm]], axis=0)
    r2 = jnp.concatenate([r2_first, r2[SUBLANES:tm]], axis=0)
    act_ref[...] = _conv_gate(a, r1, r2, b, cw_ref[...], cb_ref[...]).astype(act_ref.dtype)
    last = a[tm - SUBLANES:tm]
    prev_ref[...] = last
    tail_ref[...] = last


def _ffn_up_sample_kernel(h_ref, wa_ref, wb_ref, cw_ref, cb_ref, s0_ref, s1_ref, act_ref, a_ref,
                          wabf, wbbf):
    _cast_weights(wa_ref, wb_ref, wabf, wbbf)
    hv = h_ref[...]
    a = _dot(hv, wabf[...])
    b = _dot(hv, wbbf[...])
    t = lax.broadcasted_iota(jnp.int32, a.shape, 0) % DEC_T
    s0, s1 = s0_ref[...], s1_ref[...]
    r1 = jnp.where(t == 0, s1, pltpu.roll(a, 1, axis=0))
    r2 = jnp.where(t == 0, s0, jnp.where(t == 1, s1, pltpu.roll(a, 2, axis=0)))
    act_ref[...] = _conv_gate(a, r1, r2, b, cw_ref[...], cb_ref[...]).astype(act_ref.dtype)
    a_ref[...] = a


def _ffn_up_specs(tm, k, nj):
    tn = TN_FF
    return [pl.BlockSpec((tm, k), lambda j, i: (i, 0)),
            pl.BlockSpec((k, tn), lambda j, i: (0, j)),
            pl.BlockSpec((k, tn), lambda j, i: (0, j + nj)),
            pl.BlockSpec((3, tn), lambda j, i: (0, j)),
            pl.BlockSpec((1, tn), lambda j, i: (0, j))]


def _ffn_up_prompt(h_bf, w_up, conv_w, conv_b, *, tm):
    m, k = h_bf.shape
    tn = TN_FF
    nj, ni, tps = D_FF // tn, m // tm, SEQ // tm
    kern = functools.partial(_ffn_up_prompt_kernel, tiles_per_seq=tps)
    block_bytes = (_nbytes((tm, k), BF16) + 2 * _nbytes((k, tn), F32) + _nbytes((tm, tn), BF16)
                   + 5 * _nbytes((SUBLANES, tn), F32))
    scratch_bytes = 2 * _nbytes((k, tn), BF16) + _nbytes((SUBLANES, tn), F32)
    return pl.pallas_call(
        kern,
        name="ffn_up_conv_prompt",
        grid=(nj, ni),
        in_specs=_ffn_up_specs(tm, k, nj),
        out_specs=[pl.BlockSpec((tm, tn), lambda j, i: (i, j)),
                   pl.BlockSpec((SUBLANES, tn), lambda j, i: (i // tps, j))],
        out_shape=[jax.ShapeDtypeStruct((m, D_FF), BF16),
                   jax.ShapeDtypeStruct((m // SEQ * SUBLANES, D_FF), F32)],
        scratch_shapes=[pltpu.VMEM((k, tn), BF16), pltpu.VMEM((k, tn), BF16),
                        pltpu.VMEM((SUBLANES, tn), F32)],
        compiler_params=pltpu.CompilerParams(
            dimension_semantics=("arbitrary", "arbitrary"),
            vmem_limit_bytes=_vmem_limit(block_bytes, scratch_bytes, 8 * _nbytes((tm, tn), F32))),
    )(h_bf, w_up, w_up, conv_w, conv_b.reshape(1, D_FF))


def _ffn_up_sample(h_bf, w_up, conv_w, conv_b, state):
    m, k = h_bf.shape
    tn = TN_FF
    nj = D_FF // tn
    s0 = jnp.repeat(state[:, 0], DEC_T, axis=0)
    s1 = jnp.repeat(state[:, 1], DEC_T, axis=0)
    block_bytes = (_nbytes((m, k), BF16) + 2 * _nbytes((k, tn), F32) + _nbytes((m, tn), BF16)
                   + 3 * _nbytes((m, tn), F32))
    scratch_bytes = 2 * _nbytes((k, tn), BF16)
    tile = pl.BlockSpec((m, tn), lambda j, i: (0, j))
    return pl.pallas_call(
        _ffn_up_sample_kernel,
        name="ffn_up_conv_sample",
        grid=(nj, 1),
        in_specs=_ffn_up_specs(m, k, nj) + [tile, tile],
        out_specs=[tile, tile],
        out_shape=[jax.ShapeDtypeStruct((m, D_FF), BF16), jax.ShapeDtypeStruct((m, D_FF), F32)],
        scratch_shapes=[pltpu.VMEM((k, tn), BF16), pltpu.VMEM((k, tn), BF16)],
        compiler_params=pltpu.CompilerParams(
            dimension_semantics=("arbitrary", "arbitrary"),
            vmem_limit_bytes=_vmem_limit(block_bytes, scratch_bytes, 8 * _nbytes((m, tn), F32))),
    )(h_bf, w_up, w_up, conv_w, conv_b.reshape(1, D_FF), s0, s1)


TN = 512


def _trunk(tag, x_f32, attn_a, attn_b, gates, w, *, tm, ffn_up_fn):
    gated = _merge("merge_" + tag, attn_a, attn_b, w["w_proj_a"], w["w_proj_b"], gates, tm=tm, tn=TN)
    (s1,) = _matmul("out_proj_" + tag, gated, w["w_out"], col0=0, ncols=D_MODEL, tm=tm, tn=TN,
                    out_dtypes=(F32,), epilogue=_ep_residual, extras=(x_f32,), extra_col0=(0,))
    h_f32, h_bf = _layer_norm("ln1_" + tag, s1, w["ln1_g"], w["ln1_b"], tm=256)
    act, a_info = ffn_up_fn(h_bf)
    (s2,) = _matmul("ffn_down_" + tag, act, w["w_ffn_down_bf"], col0=0, ncols=D_MODEL,
                    tm=min(tm, 512), tn=TN, out_dtypes=(F32,), epilogue=_ep_residual,
                    extras=(h_f32,), extra_col0=(0,))
    y, _ = _layer_norm("ln2_" + tag, s2, w["ln2_g"], w["ln2_b"], tm=256)
    return y, a_info


def kernel(x_prompt, x_sample, cache_diff_k, cache_diff_v, cache_fox_k, cache_fox_v, cache_fox_logf,
           state_ffn_conv, page_table, w_in, b_f, lambda_q1, lambda_k1, lambda_q2, lambda_k2, subln_g,
           w_proj_a, w_proj_b, w_out, ln1_g, ln1_b, w_ffn_up, conv_w, conv_b, w_ffn_down, ln2_g, ln2_b):
    n_batch, seq, d_model = x_prompt.shape
    n_seq, dec_t, _ = x_sample.shape
    assert (seq, d_model, dec_t) == (SEQ, D_MODEL, DEC_T) and w_in.shape[0] == DEPTH == 1
    mp, ms = n_batch * seq, n_seq * dec_t

    w_in_t = jnp.transpose(w_in[0], (1, 0))
    off_f = 6 * WIDTH
    off_g = off_f + N_HEADS
    w_f_t = w_in_t[off_f:off_g]
    w_gate_t = w_in_t[off_g:]
    w = dict(w_proj_a=w_proj_a[0], w_proj_b=w_proj_b[0], w_out=w_out[0], ln1_g=ln1_g[0], ln1_b=ln1_b[0],
             w_ffn_down_bf=w_ffn_down[0].astype(BF16), ln2_g=ln2_g[0], ln2_b=ln2_b[0])
    w_up = w_ffn_up[0]
    lams = tuple(v.reshape(1, DK_A) for v in (lambda_q1[0], lambda_k1[0], lambda_q2[0], lambda_k2[0]))

    xp = x_prompt.reshape(mp, d_model)
    xs = x_sample.reshape(ms, d_model)
    xp_bf, xs_bf = xp.astype(BF16), xs.astype(BF16)

    def project(tag, x_bf, tm, with_bf_copy):
        res = {}
        names = ("qa", "ka", "va", "qb", "kb", "vb")
        for idx, name in enumerate(names):
            common = dict(col0=idx * WIDTH, ncols=WIDTH, tm=tm, tn=TN, w_t=True)
            call = "proj_%s_%s" % (name, tag)
            if name[0] == "q":
                (res[name],) = _matmul(call, x_bf, w_in_t, out_dtypes=(BF16,), epilogue=_ep_identity, **common)
            elif with_bf_copy:
                res[name], res[name + "_bf"] = _matmul(call, x_bf, w_in_t, out_dtypes=(F32, BF16),
                                                       epilogue=_ep_dup, **common)
            else:
                (res[name],) = _matmul(call, x_bf, w_in_t, out_dtypes=(F32,), epilogue=_ep_identity, **common)
        (res["gates"],) = _matmul("proj_gates_" + tag, x_bf, w_gate_t, col0=0, ncols=2 * D_MODEL, tm=tm,
                                  tn=TN, out_dtypes=(F32,), epilogue=_ep_sigmoid, w_t=True)
        return res

    pp = project("prompt", xp_bf, 1024, True)
    ps = project("sample", xs_bf, ms, False)

    tmf = 512
    tri = (jnp.arange(tmf)[:, None] <= jnp.arange(tmf)[None, :])
    logf_p, csum_p = _fgate("fgate_prompt", xp_bf, w_f_t, b_f[0], tri.astype(BF16), n_seq=n_batch,
                            tiles_per_seq=seq // tmf, tm=tmf, carry_tiles=True)
    grp = jnp.arange(ms) // dec_t
    blockdiag = tri[:ms, :ms] & (grp[:, None] == grp[None, :])
    logf_s, csum_s = _fgate("fgate_sample", xs_bf, w_f_t, b_f[0], blockdiag.astype(BF16), n_seq=1,
                            tiles_per_seq=1, tm=ms, carry_tiles=False)

    oa_p = _diff_prompt(pp["qa"], pp["ka_bf"], pp["va_bf"], lams, subln_g[0], n_batch)
    csum_rows = csum_p.reshape(N_HEADS, n_batch, 1, seq).transpose(1, 0, 2, 3)
    ob_p = _fox_prompt(pp["qb"], pp["kb_bf"], pp["vb_bf"], csum_rows, n_batch)

    n_pool = cache_diff_k.shape[1]
    kt_a = jnp.transpose(cache_diff_k[0], (0, 2, 3, 4, 1)).reshape(n_pool, WIDTH, PAGE)
    grouped = (n_pool, PAGE, HEAD_GROUPS, SUBLANES, HEAD_W)
    pools = (kt_a, cache_diff_v[0].reshape(grouped), cache_fox_k[0].reshape(grouped),
             cache_fox_v[0].reshape(grouped))
    suffix = _page_suffix_sums(cache_fox_logf[0])
    qa = ps["qa"].astype(F32).reshape(n_seq, dec_t, N_HEADS, 2, DK_A) * (DK_A ** -0.5)
    qa = qa.transpose(0, 2, 3, 1, 4)
    eye = jnp.eye(2, dtype=F32)[None, None, :, None, :, None]
    qa8 = (qa[:, :, :, :, None, :] * eye).reshape(n_seq, ROWS, HEAD_W)
    qb = ps["qb"].astype(F32).reshape(n_seq, dec_t, N_HEADS, HEAD_W).transpose(0, 2, 1, 3)
    qb8 = jnp.concatenate([qb, jnp.zeros_like(qb)], axis=2).reshape(n_seq, ROWS, HEAD_W)
    new_kv = tuple(ps[nm].reshape(n_seq, dec_t, WIDTH) for nm in ("ka", "va", "kb", "vb"))
    cnew = csum_s.reshape(N_HEADS, n_seq, dec_t).transpose(1, 0, 2)
    cnew = jnp.zeros((n_seq, N_HEADS, LANES), F32).at[:, :, :dec_t].set(cnew)
    oa_s, ob_s = _decode_attention(page_table, lams, subln_g[0], qa8, qb8, pools, suffix, new_kv, cnew)
    oa_s = oa_s.reshape(ms, WIDTH).astype(BF16)
    ob_s = ob_s.reshape(ms, WIDTH).astype(BF16)

    conv_w0, conv_b0 = conv_w[0], conv_b[0]
    y_p, tail_p = _trunk("prompt", xp, oa_p, ob_p, pp["gates"], w, tm=1024,
                         ffn_up_fn=lambda h: _ffn_up_prompt(h, w_up, conv_w0, conv_b0, tm=1024))
    y_s, a_s = _trunk("sample", xs, oa_s, ob_s, ps["gates"], w, tm=ms,
                      ffn_up_fn=lambda h: _ffn_up_sample(h, w_up, conv_w0, conv_b0, state_ffn_conv[0]))

    def heads(x, n, t, *tail):
        return x.reshape(1, n, t, N_HEADS, *tail)

    conv_p = tail_p.reshape(n_batch, SUBLANES, D_FF)[None, :, SUBLANES - 2:, :]
    conv_s = a_s.reshape(n_seq, dec_t, D_FF)[None, :, dec_t - 2:, :]
    return (y_p.reshape(n_batch, seq, d_model), y_s.reshape(n_seq, dec_t, d_model),
            heads(pp["ka"], n_batch, seq, 2, DK_A), heads(pp["va"], n_batch, seq, HEAD_W),
            heads(pp["kb"], n_batch, seq, HEAD_W), heads(pp["vb"], n_batch, seq, HEAD_W),
            heads(logf_p, n_batch, seq), conv_p,
            heads(ps["ka"], n_seq, dec_t, 2, DK_A), heads(ps["va"], n_seq, dec_t, HEAD_W),
            heads(ps["kb"], n_seq, dec_t, HEAD_W), heads(ps["vb"], n_seq, dec_t, HEAD_W),
            heads(logf_s, n_seq, dec_t), conv_s)
```

```python
import functools
import math

import jax
import jax.numpy as jnp
from jax import lax
from jax.experimental import pallas as pl
from jax.experimental.pallas import tpu as pltpu

F32 = jnp.float32
BF16 = jnp.bfloat16

V7X_VMEM_CAP = 60000 * 1024
LANES = 128
SUBLANES = 8

D_MODEL = 4096
SEQ = 2048
PAGE = 128
N_HEADS = 16
HEAD_W = 128
DK_A = 64
WIDTH = N_HEADS * HEAD_W
D_FF = 11008
DEPTH = 1
ALPHA = (2.0 * DEPTH) ** 0.25
LN_EPS = 1e-5
RMS_EPS = 1e-5
LAM_INIT = 0.8 - 0.6 * math.exp(-0.3 * 0)
NEG_INF = float("-inf")


def _vmem_limit(block_bytes, scratch_bytes=0, temp_bytes=0):
    need = 2 * block_bytes + scratch_bytes + temp_bytes + (2 << 20)
    return int(min(max(need, 16 << 20), V7X_VMEM_CAP))


def _nbytes(shape, dtype):
    return math.prod(shape) * jnp.dtype(dtype).itemsize


def _dot(a, b):
    return jnp.dot(a, b, preferred_element_type=F32)


def _dot_nt(a, b):
    return lax.dot_general(a, b, (((1,), (1,)), ((), ())), preferred_element_type=F32)


def _mm_kernel(*refs, n_extra, n_out, epilogue, cast_w, w_t):
    a_ref, w_ref = refs[0], refs[1]
    extra = refs[2:2 + n_extra]
    outs = refs[2 + n_extra:2 + n_extra + n_out]
    if cast_w:
        wbf_ref = refs[2 + n_extra + n_out]

        @pl.when(pl.program_id(1) == 0)
        def _():
            wbf_ref[...] = w_ref[...].astype(BF16)

        w = wbf_ref[...]
    else:
        w = w_ref[...]
    acc = _dot_nt(a_ref[...], w) if w_t else _dot(a_ref[...], w)
    vals = epilogue(acc, *[e[...] for e in extra])
    for o, v in zip(outs, vals):
        o[...] = v.astype(o.dtype)


def _matmul(name, a, w, *, col0, ncols, tm, tn, out_dtypes, epilogue, extras=(), extra_col0=(), w_t=False):
    m, k = a.shape
    assert m % tm == 0 and ncols % tn == 0 and col0 % tn == 0
    nj, ni = ncols // tn, m // tm
    jb = col0 // tn
    cast_w = w.dtype != BF16
    w_block = (tn, k) if w_t else (k, tn)
    w_map = (lambda j, i: (j + jb, 0)) if w_t else (lambda j, i: (0, j + jb))
    in_specs = [pl.BlockSpec((tm, k), lambda j, i: (i, 0)), pl.BlockSpec(w_block, w_map)]
    block_bytes = _nbytes((tm, k), a.dtype) + _nbytes(w_block, w.dtype)
    for e, c0 in zip(extras, extra_col0):
        assert c0 % tn == 0
        eb = c0 // tn
        in_specs.append(pl.BlockSpec((tm, tn), lambda j, i, eb=eb: (i, j + eb)))
        block_bytes += _nbytes((tm, tn), e.dtype)
    out_shape = [jax.ShapeDtypeStruct((m, ncols), d) for d in out_dtypes]
    out_specs = [pl.BlockSpec((tm, tn), lambda j, i: (i, j)) for _ in out_dtypes]
    for d in out_dtypes:
        block_bytes += _nbytes((tm, tn), d)
    scratch = [pltpu.VMEM(w_block, BF16)] if cast_w else []
    scratch_bytes = _nbytes(w_block, BF16) if cast_w else 0
    kern = functools.partial(_mm_kernel, n_extra=len(extras), n_out=len(out_dtypes),
                             epilogue=epilogue, cast_w=cast_w, w_t=w_t)
    return pl.pallas_call(
        kern,
        name=name,
        grid=(nj, ni),
        in_specs=in_specs,
        out_specs=out_specs,
        out_shape=out_shape,
        scratch_shapes=scratch,
        compiler_params=pltpu.CompilerParams(
            dimension_semantics=("arbitrary", "arbitrary"),
            vmem_limit_bytes=_vmem_limit(block_bytes, scratch_bytes, 2 * _nbytes((tm, tn), F32))),
    )(a, w, *extras)


def _ep_identity(acc):
    return (acc,)


def _ep_dup(acc):
    return (acc, acc)


def _ep_sigmoid(acc):
    return (jax.nn.sigmoid(acc),)


def _ep_residual(acc, res):
    return (ALPHA * res + acc,)


def _merge_kernel(oa_ref, ob_ref, wa_ref, wb_ref, ga_ref, gb_ref, out_ref, wabf, wbbf):
    @pl.when(pl.program_id(1) == 0)
    def _():
        wabf[...] = wa_ref[...].astype(BF16)
        wbbf[...] = wb_ref[...].astype(BF16)

    ba = _dot(oa_ref[...], wabf[...])
    bb = _dot(ob_ref[...], wbbf[...])
    out_ref[...] = (ga_ref[...] * ba + gb_ref[...] * bb).astype(out_ref.dtype)


def _merge(name, oa, ob, wa, wb, gates, *, tm, tn):
    m, k = oa.shape
    n = wa.shape[1]
    nj, ni = n // tn, m // tm
    gb0 = n // tn
    block_bytes = (2 * _nbytes((tm, k), BF16) + 2 * _nbytes((k, tn), F32)
                   + 2 * _nbytes((tm, tn), F32) + _nbytes((tm, tn), BF16))
    return pl.pallas_call(
        _merge_kernel,
        name=name,
        grid=(nj, ni),
        in_specs=[pl.BlockSpec((tm, k), lambda j, i: (i, 0)),
                  pl.BlockSpec((tm, k), lambda j, i: (i, 0)),
                  pl.BlockSpec((k, tn), lambda j, i: (0, j)),
                  pl.BlockSpec((k, tn), lambda j, i: (0, j)),
                  pl.BlockSpec((tm, tn), lambda j, i: (i, j)),
                  pl.BlockSpec((tm, tn), lambda j, i: (i, j + gb0))],
        out_specs=pl.BlockSpec((tm, tn), lambda j, i: (i, j)),
        out_shape=jax.ShapeDtypeStruct((m, n), BF16),
        scratch_shapes=[pltpu.VMEM((k, tn), BF16), pltpu.VMEM((k, tn), BF16)],
        compiler_params=pltpu.CompilerParams(
            dimension_semantics=("arbitrary", "arbitrary"),
            vmem_limit_bytes=_vmem_limit(block_bytes, 2 * _nbytes((k, tn), BF16),
                                         4 * _nbytes((tm, tn), F32))),
    )(oa, ob, wa, wb, gates, gates)


def _ln_kernel(x_ref, g_ref, b_ref, of_ref, ob_ref):
    x = x_ref[...]
    mu = jnp.mean(x, axis=-1, keepdims=True)
    xc = x - mu
    var = jnp.mean(xc * xc, axis=-1, keepdims=True)
    y = xc * lax.rsqrt(var + LN_EPS) * g_ref[...] + b_ref[...]
    of_ref[...] = y
    ob_ref[...] = y.astype(BF16)


def _layer_norm(name, x, g, b, *, tm):
    m, d = x.shape
    block_bytes = _nbytes((tm, d), F32) * 2 + _nbytes((tm, d), BF16)
    return pl.pallas_call(
        _ln_kernel,
        name=name,
        grid=(m // tm,),
        in_specs=[pl.BlockSpec((tm, d), lambda i: (i, 0)),
                  pl.BlockSpec((1, d), lambda i: (0, 0)),
                  pl.BlockSpec((1, d), lambda i: (0, 0))],
        out_specs=[pl.BlockSpec((tm, d), lambda i: (i, 0)),
                   pl.BlockSpec((tm, d), lambda i: (i, 0))],
        out_shape=[jax.ShapeDtypeStruct((m, d), F32), jax.ShapeDtypeStruct((m, d), BF16)],
        compiler_params=pltpu.CompilerParams(
            dimension_semantics=("arbitrary",),
            vmem_limit_bytes=_vmem_limit(block_bytes, 0, 3 * _nbytes((tm, d), F32))),
    )(x, g.reshape(1, d), b.reshape(1, d))


def _log_sigmoid(x):
    return jnp.minimum(x, 0.0) - jnp.log1p(jnp.exp(-jnp.abs(x)))


def _split3(x):
    hi = x.astype(BF16)
    r1 = x - hi.astype(F32)
    mid = r1.astype(BF16)
    lo = (r1 - mid.astype(F32)).astype(BF16)
    return hi, mid, lo


def _dot_exact01(x, ones_bf16):
    hi, mid, lo = _split3(x)
    return _dot(hi, ones_bf16) + _dot(mid, ones_bf16) + _dot(lo, ones_bf16)


def _fgate_kernel(x_ref, wf_ref, wft_ref, bf_row_ref, bf_col_ref, u_ref,
                  logf_ref, csum_ref, carry_ref, *, carry_tiles):
    x = x_ref[...]
    f_nat = _dot(x, wf_ref[...])
    logf_ref[...] = _log_sigmoid(f_nat[:, :N_HEADS] + bf_row_ref[...])
    f_t = _dot_nt(wft_ref[...], x)
    logf_t = _log_sigmoid(f_t + bf_col_ref[...])
    c = _dot_exact01(logf_t, u_ref[...])
    if carry_tiles:
        t = pl.program_id(1)

        @pl.when(t == 0)
        def _():
            carry_ref[...] = jnp.zeros_like(carry_ref)

        c = c + carry_ref[:, 0:1]
        carry_ref[...] = jnp.broadcast_to(c[:, -1:], carry_ref.shape)
    csum_ref[...] = c


def _fgate(name, x_bf, w_f_t, b_f, u_mat, *, n_seq, tiles_per_seq, tm, carry_tiles):
    m, k = x_bf.shape
    wft = w_f_t.astype(BF16)
    wf_pad = jnp.zeros((k, LANES), BF16).at[:, :N_HEADS].set(wft.T)
    bf_row = b_f.reshape(1, N_HEADS)
    bf_col = b_f.reshape(N_HEADS, 1)
    kern = functools.partial(_fgate_kernel, carry_tiles=carry_tiles)
    tps = tiles_per_seq
    block_bytes = (_nbytes((tm, k), BF16) + _nbytes((k, LANES), BF16) + _nbytes((N_HEADS, k), BF16)
                   + _nbytes((tm, tm), BF16) + _nbytes((tm, LANES), F32) + _nbytes((N_HEADS, tm), F32))
    return pl.pallas_call(
        kern,
        name=name,
        grid=(n_seq, tps),
        in_specs=[pl.BlockSpec((tm, k), lambda n, t: (n * tps + t, 0)),
                  pl.BlockSpec((k, LANES), lambda n, t: (0, 0)),
                  pl.BlockSpec((N_HEADS, k), lambda n, t: (0, 0)),
                  pl.BlockSpec((1, N_HEADS), lambda n, t: (0, 0)),
                  pl.BlockSpec((N_HEADS, 1), lambda n, t: (0, 0)),
                  pl.BlockSpec((tm, tm), lambda n, t: (0, 0))],
        out_specs=[pl.BlockSpec((tm, N_HEADS), lambda n, t: (n * tps + t, 0)),
                   pl.BlockSpec((N_HEADS, tm), lambda n, t: (0, n * tps + t))],
        out_shape=[jax.ShapeDtypeStruct((m, N_HEADS), F32),
                   jax.ShapeDtypeStruct((N_HEADS, m), F32)],
        scratch_shapes=[pltpu.VMEM((N_HEADS, LANES), F32)],
        compiler_params=pltpu.CompilerParams(
            dimension_semantics=("arbitrary", "arbitrary"),
            vmem_limit_bytes=_vmem_limit(block_bytes, 0, 8 * _nbytes((tm, LANES), F32))),
    )(x_bf, wf_pad, wft, bf_row, bf_col, u_mat)


def _suffix_kernel(x_ref, m_ref, o_ref):
    o_ref[...] = _dot_exact01(x_ref[...], m_ref[...])


def _page_suffix_sums(logf_pool):
    n_pool = logf_pool.shape[0]
    rows = n_pool * N_HEADS
    x = jnp.transpose(logf_pool, (0, 2, 1)).reshape(rows, PAGE)
    sel = (jnp.arange(PAGE)[:, None] >= jnp.arange(PAGE)[None, :]).astype(BF16)
    tr = 4096
    assert rows % tr == 0
    block_bytes = 2 * _nbytes((tr, PAGE), F32) + _nbytes((PAGE, PAGE), BF16)
    out = pl.pallas_call(
        _suffix_kernel,
        name="page_suffix_sums",
        grid=(rows // tr,),
        in_specs=[pl.BlockSpec((tr, PAGE), lambda i: (i, 0)),
                  pl.BlockSpec((PAGE, PAGE), lambda i: (0, 0))],
        out_specs=pl.BlockSpec((tr, PAGE), lambda i: (i, 0)),
        out_shape=jax.ShapeDtypeStruct((rows, PAGE), F32),
        compiler_params=pltpu.CompilerParams(
            dimension_semantics=("arbitrary",),
            vmem_limit_bytes=_vmem_limit(block_bytes, 0, 6 * _nbytes((tr, PAGE), F32))),
    )(x, sel)
    return out.reshape(n_pool, N_HEADS, PAGE)


TQ = 256
NQB = SEQ // TQ


def _lambda_value(lq1, lk1, lq2, lk2):
    s1 = jnp.sum(lq1[...] * lk1[...], axis=-1, keepdims=True)
    s2 = jnp.sum(lq2[...] * lk2[...], axis=-1, keepdims=True)
    return jnp.exp(s1) - jnp.exp(s2) + LAM_INIT


def _causal_strip():
    r = lax.broadcasted_iota(jnp.int32, (TQ, SEQ), 0)
    j = lax.broadcasted_iota(jnp.int32, (TQ, SEQ), 1)
    return r - j + (SEQ - TQ)


def _diff_prompt_kernel(lq1, lk1, lq2, lk2, g_ref, q_ref, k_ref, v_ref, o_ref, bias_ref):
    h = pl.program_id(1)
    lam = _lambda_value(lq1, lk1, lq2, lk2)
    slope = jnp.exp2(jnp.full((1, 1), -0.5, F32) * (h + 1).astype(F32))
    d = _causal_strip()
    bias_ref[...] = jnp.where(d >= 0, -slope * d.astype(F32), NEG_INF)
    lane = lax.broadcasted_iota(jnp.int32, (TQ, HEAD_W), 1)
    g = g_ref[...]
    for b in range(NQB):
        s_len = (b + 1) * TQ
        q = q_ref[b * TQ:(b + 1) * TQ, :] * jnp.asarray(DK_A ** -0.5, BF16)
        q1 = jnp.where(lane < DK_A, q, jnp.zeros_like(q))
        q2 = jnp.where(lane >= DK_A, q, jnp.zeros_like(q))
        k = k_ref[0:s_len, :]
        v = v_ref[0:s_len, :]
        c0 = (NQB - 1 - b) * TQ
        bias = bias_ref[:, c0:c0 + s_len]
        s = _dot_nt(jnp.concatenate([q1, q2], axis=0), k) + jnp.concatenate([bias, bias], axis=0)
        m = jnp.max(s, axis=-1, keepdims=True)
        e = jnp.exp(s - m)
        l = jnp.sum(e, axis=-1, keepdims=True)
        pv = _dot(e.astype(BF16), v) / l
        o = pv[0:TQ] - lam * pv[TQ:2 * TQ]
        o = o * lax.rsqrt(jnp.mean(o * o, axis=-1, keepdims=True) + RMS_EPS) * g
        o_ref[b * TQ:(b + 1) * TQ, :] = (o * (1.0 - LAM_INIT)).astype(o_ref.dtype)


def _fox_prompt_kernel(q_ref, k_ref, v_ref, c_ref, o_ref, mask_ref):
    d = _causal_strip()
    mask_ref[...] = jnp.where(d >= 0, 0.0, NEG_INF).astype(F32)
    scale = HEAD_W ** -0.5
    for b in range(NQB):
        s_len = (b + 1) * TQ
        q = q_ref[b * TQ:(b + 1) * TQ, :]
        k = k_ref[0:s_len, :]
        v = v_ref[0:s_len, :]
        c0 = (NQB - 1 - b) * TQ
        bias = mask_ref[:, c0:c0 + s_len] - c_ref[:, 0:s_len]
        s = _dot_nt(q, k) * scale + bias
        m = jnp.max(s, axis=-1, keepdims=True)
        e = jnp.exp(s - m)
        l = jnp.sum(e, axis=-1, keepdims=True)
        o_ref[b * TQ:(b + 1) * TQ, :] = (_dot(e.astype(BF16), v) / l).astype(o_ref.dtype)


def _head_spec():
    return pl.BlockSpec((SEQ, HEAD_W), lambda n, h: (n, h))


def _small_spec(shape):
    return pl.BlockSpec(shape, lambda n, h: (0,) * len(shape))


_ATT_TEMP = (4 * 4 + 2) * 2 * TQ * SEQ


def _diff_prompt(q, k, v, lams, subln_g, n_batch):
    block_bytes = 4 * _nbytes((SEQ, HEAD_W), BF16)
    return pl.pallas_call(
        _diff_prompt_kernel,
        name="diff_prompt_attention",
        grid=(n_batch, N_HEADS),
        in_specs=[_small_spec((1, DK_A))] * 4 + [_small_spec((1, HEAD_W))] + [_head_spec()] * 3,
        out_specs=_head_spec(),
        out_shape=jax.ShapeDtypeStruct(q.shape, BF16),
        scratch_shapes=[pltpu.VMEM((TQ, SEQ), F32)],
        compiler_params=pltpu.CompilerParams(
            dimension_semantics=("arbitrary", "arbitrary"),
            vmem_limit_bytes=_vmem_limit(block_bytes, _nbytes((TQ, SEQ), F32), _ATT_TEMP)),
    )(*lams, subln_g.reshape(1, HEAD_W), q, k, v)


def _fox_prompt(q, k, v, csum, n_batch):
    block_bytes = 4 * _nbytes((SEQ, HEAD_W), BF16) + _nbytes((8, SEQ), F32)
    return pl.pallas_call(
        _fox_prompt_kernel,
        name="fox_prompt_attention",
        grid=(n_batch, N_HEADS),
        in_specs=[_head_spec()] * 3 + [pl.BlockSpec((None, None, 1, SEQ), lambda n, h: (n, h, 0, 0))],
        out_specs=_head_spec(),
        out_shape=jax.ShapeDtypeStruct(q.shape, BF16),
        scratch_shapes=[pltpu.VMEM((TQ, SEQ), F32)],
        compiler_params=pltpu.CompilerParams(
            dimension_semantics=("arbitrary", "arbitrary"),
            vmem_limit_bytes=_vmem_limit(block_bytes, _nbytes((TQ, SEQ), F32), _ATT_TEMP)),
    )(q, k, v, csum)


N_PAGES = SEQ // PAGE
PPS = 4
N_STEPS = N_PAGES // PPS
DEC_T = 4
ROWS = N_HEADS * SUBLANES
HEAD_GROUPS = N_HEADS // SUBLANES


def _expand_heads(x):
    return jnp.concatenate(
        [jnp.broadcast_to(x[h:h + 1, :], (SUBLANES, x.shape[1])) for h in range(N_HEADS)], axis=0)


def _head_rows(x, h):
    return x[h * SUBLANES:(h + 1) * SUBLANES, :]


def _online_update(s, v_of_head, m_ref, l_ref, acc_ref):
    m_old = m_ref[...]
    m_new = jnp.maximum(m_old, jnp.max(s, axis=-1, keepdims=True))
    e = jnp.exp(s - m_new)
    alpha = jnp.exp(m_old - m_new)
    l_ref[...] = alpha * l_ref[...] + jnp.sum(e, axis=-1, keepdims=True)
    pv = jnp.concatenate([_dot(_head_rows(e, h), v_of_head(h)) for h in range(N_HEADS)], axis=0)
    acc_ref[...] = alpha * acc_ref[...] + pv
    m_ref[...] = m_new


def _decode_kernel(pt_ref, lq1, lk1, lq2, lk2, g_ref, qa_ref, qb_ref, *rest):
    del pt_ref
    n_blk = PPS * HEAD_GROUPS
    kta = rest[0:PPS]
    va, kb, vb = (rest[PPS + i * n_blk:PPS + (i + 1) * n_blk] for i in range(3))
    suf = rest[PPS + 3 * n_blk:2 * PPS + 3 * n_blk]
    (kna_ref, vna_ref, knb_ref, vnb_ref, cnew_ref, oa_ref, ob_ref,
     ma, la, acca, mb, lb, accb, carry, pka, pva, pkb, pvb) = rest[2 * PPS + 3 * n_blk:]
    b = pl.program_id(0)
    step = pl.program_id(1)
    row = lax.broadcasted_iota(jnp.int32, (ROWS, LANES), 0)
    lane = lax.broadcasted_iota(jnp.int32, (ROWS, LANES), 1)
    head_f = (row // SUBLANES + 1).astype(F32)
    t_row = row % DEC_T
    slope = jnp.exp2(-0.5 * head_f)
    fox_scale = HEAD_W ** -0.5

    @pl.when((b == 0) & (step == 0))
    def _():
        for p in (pka, pva, pkb, pvb):
            p[...] = jnp.zeros_like(p)

    @pl.when(step == 0)
    def _():
        for r in (ma, mb):
            r[...] = jnp.full_like(r, NEG_INF)
        for r in (la, lb, acca, accb, carry):
            r[...] = jnp.zeros_like(r)

    def strided_head(refs):
        def load(h):
            hg, h8 = divmod(h, SUBLANES)
            return jnp.concatenate(
                [refs[g * HEAD_GROUPS + hg].reshape(PAGE * SUBLANES, HEAD_W)[pl.ds(h8, PAGE, stride=SUBLANES), :]
                 for g in range(PPS)], axis=0)
        return load

    def cached_pages_step():
        q_pos = (SEQ + t_row).astype(F32)
        bias_a = []
        for g in range(PPS):
            page = N_PAGES - 1 - (step * PPS + g)
            key_pos = (page * PAGE).astype(F32) + lane.astype(F32)
            bias_a.append(-slope * (q_pos - key_pos))
        sa = jnp.concatenate(
            [_dot(qa_ref[h * SUBLANES:(h + 1) * SUBLANES, :],
                  jnp.concatenate([r[h * HEAD_W:(h + 1) * HEAD_W, :] for r in kta], axis=1))
             for h in range(N_HEADS)], axis=0)
        sa = sa + jnp.concatenate(bias_a, axis=1)
        _online_update(sa, strided_head(va), ma, la, acca)

        lane16 = lax.broadcasted_iota(jnp.int32, (N_HEADS, LANES), 1)
        c = carry[...]
        bias_b = []
        for g in range(PPS):
            sfx = suf[g][...]
            excl = jnp.where(lane16 == LANES - 1, 0.0, pltpu.roll(sfx, LANES - 1, axis=1))
            bias_b.append(_expand_heads(excl) + c)
            c = c + _expand_heads(jnp.broadcast_to(sfx[:, 0:1], (N_HEADS, LANES)))
        carry[...] = c
        k_of_head = strided_head(kb)
        sb = jnp.concatenate(
            [_dot_nt(qb_ref[h * SUBLANES:(h + 1) * SUBLANES, :], k_of_head(h)) for h in range(N_HEADS)],
            axis=0)
        sb = sb * fox_scale + jnp.concatenate(bias_b, axis=1)
        _online_update(sb, strided_head(vb), mb, lb, accb)

    def head_slab(ref):
        return lambda h: ref[:, h * HEAD_W:(h + 1) * HEAD_W]

    def new_logits(q_ref, k_ref):
        k_of_head = head_slab(k_ref)
        return jnp.concatenate(
            [_dot_nt(q_ref[h * SUBLANES:(h + 1) * SUBLANES, :], k_of_head(h)) for h in range(N_HEADS)],
            axis=0)

    cached_pages_step()

    @pl.when(step == N_STEPS - 1)
    def _():
        pka[0:DEC_T, :] = kna_ref[...]
        pva[0:DEC_T, :] = vna_ref[...]
        pkb[0:DEC_T, :] = knb_ref[...]
        pvb[0:DEC_T, :] = vnb_ref[...]
        visible = (lane <= t_row) & (lane < DEC_T)
        bias_a = jnp.where(visible, -slope * (t_row - lane).astype(F32), NEG_INF)
        _online_update(new_logits(qa_ref, pka) + bias_a, head_slab(pva), ma, la, acca)
        bias_b = jnp.where(visible, -_expand_heads(cnew_ref[...]), NEG_INF)
        _online_update(new_logits(qb_ref, pkb) * fox_scale + bias_b, head_slab(pvb), mb, lb, accb)

        lam = _lambda_value(lq1, lk1, lq2, lk2)
        na = acca[...] / la[...]
        o = na - lam * pltpu.roll(na, ROWS - DEC_T, axis=0)
        o = o * lax.rsqrt(jnp.mean(o * o, axis=-1, keepdims=True) + RMS_EPS) * g_ref[...]
        o = o * (1.0 - LAM_INIT)
        nb = accb[...] / lb[...]
        for h in range(N_HEADS):
            oa_ref[:, h * HEAD_W:(h + 1) * HEAD_W] = o[h * SUBLANES:h * SUBLANES + DEC_T, :]
            ob_ref[:, h * HEAD_W:(h + 1) * HEAD_W] = nb[h * SUBLANES:h * SUBLANES + DEC_T, :]


def _decode_attention(page_table, lams, subln_g, qa8, qb8, pools, suffix, new_kv, cnew):
    n_seq = page_table.shape[0]

    def page_of(b, s, pt, g):
        return pt[b, N_PAGES - 1 - (s * PPS + g)]

    def page_specs(shape):
        return [pl.BlockSpec((None,) + shape, lambda b, s, pt, g=g: (page_of(b, s, pt, g), 0, 0))
                for g in range(PPS)]

    def head_group_specs():
        return [pl.BlockSpec((None, PAGE, None, SUBLANES, HEAD_W),
                             lambda b, s, pt, g=g, hg=hg: (page_of(b, s, pt, g), 0, hg, 0, 0))
                for g in range(PPS) for hg in range(HEAD_GROUPS)]

    def seq_map(b, s, pt):
        return (b, 0, 0)

    def const_map(b, s, pt):
        return (0, 0)

    in_specs = ([pl.BlockSpec((1, DK_A), const_map)] * 4 + [pl.BlockSpec((1, HEAD_W), const_map)]
                + [pl.BlockSpec((None, ROWS, HEAD_W), seq_map)] * 2
                + page_specs((WIDTH, PAGE))
                + head_group_specs() * 3
                + page_specs((N_HEADS, PAGE))
                + [pl.BlockSpec((None, DEC_T, WIDTH), seq_map)] * 4
                + [pl.BlockSpec((None, N_HEADS, LANES), seq_map)])
    out_spec = pl.BlockSpec((None, DEC_T, WIDTH), seq_map)
    scratch = ([pltpu.VMEM((ROWS, 1), F32), pltpu.VMEM((ROWS, 1), F32), pltpu.VMEM((ROWS, HEAD_W), F32)] * 2
               + [pltpu.VMEM((ROWS, LANES), F32)]
               + [pltpu.VMEM((PAGE, WIDTH), F32)] * 4)
    block_bytes = 4 * PPS * _nbytes((PAGE, WIDTH), F32) + 12 * _nbytes((ROWS, LANES), F32)
    scratch_bytes = 4 * _nbytes((PAGE, WIDTH), F32) + 8 * _nbytes((ROWS, LANES), F32)
    page_args = ([pools[0]] * PPS + [p for pool in pools[1:] for p in [pool] * (PPS * HEAD_GROUPS)]
                 + [suffix] * PPS)
    return pl.pallas_call(
        _decode_kernel,
        name="paged_decode_attention",
        grid_spec=pltpu.PrefetchScalarGridSpec(
            num_scalar_prefetch=1,
            grid=(n_seq, N_STEPS),
            in_specs=in_specs,
            out_specs=[out_spec, out_spec],
            scratch_shapes=scratch),
        out_shape=[jax.ShapeDtypeStruct((n_seq, DEC_T, WIDTH), F32)] * 2,
        compiler_params=pltpu.CompilerParams(
            dimension_semantics=("arbitrary", "arbitrary"),
            vmem_limit_bytes=_vmem_limit(block_bytes, scratch_bytes,
                                         256 * _nbytes((ROWS, LANES), F32))),
    )(page_table, *lams, subln_g.reshape(1, HEAD_W), qa8, qb8, *page_args, *new_kv, cnew)


TN_FF = 256


def _conv_gate(a, r1, r2, b, cw, cb):
    conv = cb + (cw[0:1] * r2 + cw[1:2] * r1 + cw[2:3] * a)
    return jax.nn.silu(conv) * b


def _cast_weights(wa_ref, wb_ref, wabf, wbbf):
    @pl.when(pl.program_id(1) == 0)
    def _():
        wabf[...] = wa_ref[...].astype(BF16)
        wbbf[...] = wb_ref[...].astype(BF16)


def _ffn_up_prompt_kernel(h_ref, wa_ref, wb_ref, cw_ref, cb_ref, act_ref, tail_ref,
                          wabf, wbbf, prev_ref, *, tiles_per_seq):
    _cast_weights(wa_ref, wb_ref, wabf, wbbf)
    t = pl.program_id(1) % tiles_per_seq
    hv = h_ref[...]
    a = _dot(hv, wabf[...])
    b = _dot(hv, wbbf[...])
    tm = a.shape[0]
    prev = jnp.where(t == 0, 0.0, prev_ref[...])
    row8 = lax.broadcasted_iota(jnp.int32, (SUBLANES, a.shape[1]), 0)
    r1 = pltpu.roll(a, 1, axis=0)
    r2 = pltpu.roll(a, 2, axis=0)
    r1_first = jnp.where(row8 < 1, pltpu.roll(prev, 1, axis=0), r1[0:SUBLANES])
    r2_first = jnp.where(row8 < 2, pltpu.roll(prev, 2, axis=0), r2[0:SUBLANES])
    r1 = jnp.concatenate([r1_first, r1[SUBLANES:tm]], axis=0)
    r2 = jnp.concatenate([r2_first, r2[SUBLANES:tm]], axis=0)
    act_ref[...] = _conv_gate(a, r1, r2, b, cw_ref[...], cb_ref[...]).astype(act_ref.dtype)
    last = a[tm - SUBLANES:tm]
    prev_ref[...] = last
    tail_ref[...] = last


def _ffn_up_sample_kernel(h_ref, wa_ref, wb_ref, cw_ref, cb_ref, s0_ref, s1_ref, act_ref, a_ref,
                          wabf, wbbf):
    _cast_weights(wa_ref, wb_ref, wabf, wbbf)
    hv = h_ref[...]
    a = _dot(hv, wabf[...])
    b = _dot(hv, wbbf[...])
    t = lax.broadcasted_iota(jnp.int32, a.shape, 0) % DEC_T
    s0, s1 = s0_ref[...], s1_ref[...]
    r1 = jnp.where(t == 0, s1, pltpu.roll(a, 1, axis=0))
    r2 = jnp.where(t == 0, s0, jnp.where(t == 1, s1, pltpu.roll(a, 2, axis=0)))
    act_ref[...] = _conv_gate(a, r1, r2, b, cw_ref[...], cb_ref[...]).astype(act_ref.dtype)
    a_ref[...] = a


def _ffn_up_specs(tm, k, nj):
    tn = TN_FF
    return [pl.BlockSpec((tm, k), lambda j, i: (i, 0)),
            pl.BlockSpec((k, tn), lambda j, i: (0, j)),
            pl.BlockSpec((k, tn), lambda j, i: (0, j + nj)),
            pl.BlockSpec((3, tn), lambda j, i: (0, j)),
            pl.BlockSpec((1, tn), lambda j, i: (0, j))]


def _ffn_up_prompt(h_bf, w_up, conv_w, conv_b, *, tm):
    m, k = h_bf.shape
    tn = TN_FF
    nj, ni, tps = D_FF // tn, m // tm, SEQ // tm
    kern = functools.partial(_ffn_up_prompt_kernel, tiles_per_seq=tps)
    block_bytes = (_nbytes((tm, k), BF16) + 2 * _nbytes((k, tn), F32) + _nbytes((tm, tn), BF16)
                   + 5 * _nbytes((SUBLANES, tn), F32))
    scratch_bytes = 2 * _nbytes((k, tn), BF16) + _nbytes((SUBLANES, tn), F32)
    return pl.pallas_call(
        kern,
        name="ffn_up_conv_prompt",
        grid=(nj, ni),
        in_specs=_ffn_up_specs(tm, k, nj),
        out_specs=[pl.BlockSpec((tm, tn), lambda j, i: (i, j)),
                   pl.BlockSpec((SUBLANES, tn), lambda j, i: (i // tps, j))],
        out_shape=[jax.ShapeDtypeStruct((m, D_FF), BF16),
                   jax.ShapeDtypeStruct((m // SEQ * SUBLANES, D_FF), F32)],
        scratch_shapes=[pltpu.VMEM((k, tn), BF16), pltpu.VMEM((k, tn), BF16),
                        pltpu.VMEM((SUBLANES, tn), F32)],
        compiler_params=pltpu.CompilerParams(
            dimension_semantics=("arbitrary", "arbitrary"),
            vmem_limit_bytes=_vmem_limit(block_bytes, scratch_bytes, 8 * _nbytes((tm, tn), F32))),
    )(h_bf, w_up, w_up, conv_w, conv_b.reshape(1, D_FF))


def _ffn_up_sample(h_bf, w_up, conv_w, conv_b, state):
    m, k = h_bf.shape
    tn = TN_FF
    nj = D_FF // tn
    s0 = jnp.repeat(state[:, 0], DEC_T, axis=0)
    s1 = jnp.repeat(state[:, 1], DEC_T, axis=0)
    block_bytes = (_nbytes((m, k), BF16) + 2 * _nbytes((k, tn), F32) + _nbytes((m, tn), BF16)
                   + 3 * _nbytes((m, tn), F32))
    scratch_bytes = 2 * _nbytes((k, tn), BF16)
    tile = pl.BlockSpec((m, tn), lambda j, i: (0, j))
    return pl.pallas_call(
        _ffn_up_sample_kernel,
        name="ffn_up_conv_sample",
        grid=(nj, 1),
        in_specs=_ffn_up_specs(m, k, nj) + [tile, tile],
        out_specs=[tile, tile],
        out_shape=[jax.ShapeDtypeStruct((m, D_FF), BF16), jax.ShapeDtypeStruct((m, D_FF), F32)],
        scratch_shapes=[pltpu.VMEM((k, tn), BF16), pltpu.VMEM((k, tn), BF16)],
        compiler_params=pltpu.CompilerParams(
            dimension_semantics=("arbitrary", "arbitrary"),
            vmem_limit_bytes=_vmem_limit(block_bytes, scratch_bytes, 8 * _nbytes((m, tn), F32))),
    )(h_bf, w_up, w_up, conv_w, conv_b.reshape(1, D_FF), s0, s1)


TN = 512
TN_PROJ = 1024


def _trunk(tag, x_f32, attn_a, attn_b, gates, w, *, tm, ffn_up_fn):
    gated = _merge("merge_" + tag, attn_a, attn_b, w["w_proj_a"], w["w_proj_b"], gates, tm=tm, tn=TN)
    (s1,) = _matmul("out_proj_" + tag, gated, w["w_out"], col0=0, ncols=D_MODEL, tm=tm, tn=TN,
                    out_dtypes=(F32,), epilogue=_ep_residual, extras=(x_f32,), extra_col0=(0,))
    h_f32, h_bf = _layer_norm("ln1_" + tag, s1, w["ln1_g"], w["ln1_b"], tm=256)
    act, a_info = ffn_up_fn(h_bf)
    (s2,) = _matmul("ffn_down_" + tag, act, w["w_ffn_down_bf"], col0=0, ncols=D_MODEL,
                    tm=min(tm, 512), tn=TN, out_dtypes=(F32,), epilogue=_ep_residual,
                    extras=(h_f32,), extra_col0=(0,))
    y, _ = _layer_norm("ln2_" + tag, s2, w["ln2_g"], w["ln2_b"], tm=256)
    return y, a_info


def kernel(x_prompt, x_sample, cache_diff_k, cache_diff_v, cache_fox_k, cache_fox_v, cache_fox_logf,
           state_ffn_conv, page_table, w_in, b_f, lambda_q1, lambda_k1, lambda_q2, lambda_k2, subln_g,
           w_proj_a, w_proj_b, w_out, ln1_g, ln1_b, w_ffn_up, conv_w, conv_b, w_ffn_down, ln2_g, ln2_b):
    n_batch, seq, d_model = x_prompt.shape
    n_seq, dec_t, _ = x_sample.shape
    assert (seq, d_model, dec_t) == (SEQ, D_MODEL, DEC_T) and w_in.shape[0] == DEPTH == 1
    mp, ms = n_batch * seq, n_seq * dec_t

    w_in_t = jnp.transpose(w_in[0], (1, 0))
    off_f = 6 * WIDTH
    off_g = off_f + N_HEADS
    w_f_t = w_in_t[off_f:off_g]
    w_gate_t = w_in_t[off_g:]
    w = dict(w_proj_a=w_proj_a[0], w_proj_b=w_proj_b[0], w_out=w_out[0], ln1_g=ln1_g[0], ln1_b=ln1_b[0],
             w_ffn_down_bf=w_ffn_down[0].astype(BF16), ln2_g=ln2_g[0], ln2_b=ln2_b[0])
    w_up = w_ffn_up[0]
    lams = tuple(v.reshape(1, DK_A) for v in (lambda_q1[0], lambda_k1[0], lambda_q2[0], lambda_k2[0]))

    xp = x_prompt.reshape(mp, d_model)
    xs = x_sample.reshape(ms, d_model)
    xp_bf, xs_bf = xp.astype(BF16), xs.astype(BF16)

    def project(tag, x_bf, tm, with_bf_copy):
        res = {}
        names = ("qa", "ka", "va", "qb", "kb", "vb")
        for idx, name in enumerate(names):
            common = dict(col0=idx * WIDTH, ncols=WIDTH, tm=tm, tn=TN_PROJ, w_t=True)
            call = "proj_%s_%s" % (name, tag)
            if name[0] == "q":
                (res[name],) = _matmul(call, x_bf, w_in_t, out_dtypes=(BF16,), epilogue=_ep_identity, **common)
            elif with_bf_copy:
                res[name], res[name + "_bf"] = _matmul(call, x_bf, w_in_t, out_dtypes=(F32, BF16),
                                                       epilogue=_ep_dup, **common)
            else:
                (res[name],) = _matmul(call, x_bf, w_in_t, out_dtypes=(F32,), epilogue=_ep_identity, **common)
        (res["gates"],) = _matmul("proj_gates_" + tag, x_bf, w_gate_t, col0=0, ncols=2 * D_MODEL, tm=tm,
                                  tn=TN_PROJ, out_dtypes=(F32,), epilogue=_ep_sigmoid, w_t=True)
        return res

    pp = project("prompt", xp_bf, 512, True)
    ps = project("sample", xs_bf, ms, False)

    tmf = 512
    tri = (jnp.arange(tmf)[:, None] <= jnp.arange(tmf)[None, :])
    logf_p, csum_p = _fgate("fgate_prompt", xp_bf, w_f_t, b_f[0], tri.astype(BF16), n_seq=n_batch,
                            tiles_per_seq=seq // tmf, tm=tmf, carry_tiles=True)
    grp = jnp.arange(ms) // dec_t
    blockdiag = tri[:ms, :ms] & (grp[:, None] == grp[None, :])
    logf_s, csum_s = _fgate("fgate_sample", xs_bf, w_f_t, b_f[0], blockdiag.astype(BF16), n_seq=1,
                            tiles_per_seq=1, tm=ms, carry_tiles=False)

    oa_p = _diff_prompt(pp["qa"], pp["ka_bf"], pp["va_bf"], lams, subln_g[0], n_batch)
    csum_rows = csum_p.reshape(N_HEADS, n_batch, 1, seq).transpose(1, 0, 2, 3)
    ob_p = _fox_prompt(pp["qb"], pp["kb_bf"], pp["vb_bf"], csum_rows, n_batch)

    n_pool = cache_diff_k.shape[1]
    kt_a = jnp.transpose(cache_diff_k[0], (0, 2, 3, 4, 1)).reshape(n_pool, WIDTH, PAGE)
    grouped = (n_pool, PAGE, HEAD_GROUPS, SUBLANES, HEAD_W)
    pools = (kt_a, cache_diff_v[0].reshape(grouped), cache_fox_k[0].reshape(grouped),
             cache_fox_v[0].reshape(grouped))
    suffix = _page_suffix_sums(cache_fox_logf[0])
    qa = ps["qa"].astype(F32).reshape(n_seq, dec_t, N_HEADS, 2, DK_A) * (DK_A ** -0.5)
    qa = qa.transpose(0, 2, 3, 1, 4)
    eye = jnp.eye(2, dtype=F32)[None, None, :, None, :, None]
    qa8 = (qa[:, :, :, :, None, :] * eye).reshape(n_seq, ROWS, HEAD_W)
    qb = ps["qb"].astype(F32).reshape(n_seq, dec_t, N_HEADS, HEAD_W).transpose(0, 2, 1, 3)
    qb8 = jnp.concatenate([qb, jnp.zeros_like(qb)], axis=2).reshape(n_seq, ROWS, HEAD_W)
    new_kv = tuple(ps[nm].reshape(n_seq, dec_t, WIDTH) for nm in ("ka", "va", "kb", "vb"))
    cnew = csum_s.reshape(N_HEADS, n_seq, dec_t).transpose(1, 0, 2)
    cnew = jnp.zeros((n_seq, N_HEADS, LANES), F32).at[:, :, :dec_t].set(cnew)
    oa_s, ob_s = _decode_attention(page_table, lams, subln_g[0], qa8, qb8, pools, suffix, new_kv, cnew)
    oa_s = oa_s.reshape(ms, WIDTH).astype(BF16)
    ob_s = ob_s.reshape(ms, WIDTH).astype(BF16)

    conv_w0, conv_b0 = conv_w[0], conv_b[0]
    y_p, tail_p = _trunk("prompt", xp, oa_p, ob_p, pp["gates"], w, tm=1024,
                         ffn_up_fn=lambda h: _ffn_up_prompt(h, w_up, conv_w0, conv_b0, tm=1024))
    y_s, a_s = _trunk("sample", xs, oa_s, ob_s, ps["gates"], w, tm=ms,
                      ffn_up_fn=lambda h: _ffn_up_sample(h, w_up, conv_w0, conv_b0, state_ffn_conv[0]))

    def heads(x, n, t, *tail):
        return x.reshape(1, n, t, N_HEADS, *tail)

    conv_p = tail_p.reshape(n_batch, SUBLANES, D_FF)[None, :, SUBLANES - 2:, :]
    conv_s = a_s.reshape(n_seq, dec_t, D_FF)[None, :, dec_t - 2:, :]
    return (y_p.reshape(n_batch, seq, d_model), y_s.reshape(n_seq, dec_t, d_model),
            heads(pp["ka"], n_batch, seq, 2, DK_A), heads(pp["va"], n_batch, seq, HEAD_W),
            heads(pp["kb"], n_batch, seq, HEAD_W), heads(pp["vb"], n_batch, seq, HEAD_W),
            heads(logf_p, n_batch, seq), conv_p,
            heads(ps["ka"], n_seq, dec_t, 2, DK_A), heads(ps["va"], n_seq, dec_t, HEAD_W),
            heads(ps["kb"], n_seq, dec_t, HEAD_W), heads(ps["vb"], n_seq, dec_t, HEAD_W),
            heads(logf_s, n_seq, dec_t), conv_s)
```

```python
import functools
import math

import jax
import jax.numpy as jnp
from jax import lax
from jax.experimental import pallas as pl
from jax.experimental.pallas import tpu as pltpu

F32 = jnp.float32
BF16 = jnp.bfloat16

V7X_VMEM_CAP = 60000 * 1024
LANES = 128
SUBLANES = 8

D_MODEL = 4096
SEQ = 2048
PAGE = 128
N_HEADS = 16
HEAD_W = 128
DK_A = 64
WIDTH = N_HEADS * HEAD_W
D_FF = 11008
DEPTH = 1
ALPHA = (2.0 * DEPTH) ** 0.25
LN_EPS = 1e-5
RMS_EPS = 1e-5
LAM_INIT = 0.8 - 0.6 * math.exp(-0.3 * 0)
NEG_INF = float("-inf")


def _vmem_limit(block_bytes, scratch_bytes=0, temp_bytes=0):
    need = 2 * block_bytes + scratch_bytes + temp_bytes + (2 << 20)
    return int(min(max(need, 16 << 20), V7X_VMEM_CAP))


def _nbytes(shape, dtype):
    return math.prod(shape) * jnp.dtype(dtype).itemsize


def _dot(a, b):
    return jnp.dot(a, b, preferred_element_type=F32)


def _dot_nt(a, b):
    return lax.dot_general(a, b, (((1,), (1,)), ((), ())), preferred_element_type=F32)


def _mm_kernel(*refs, n_extra, n_out, epilogue, cast_w, w_t):
    a_ref, w_ref = refs[0], refs[1]
    extra = refs[2:2 + n_extra]
    outs = refs[2 + n_extra:2 + n_extra + n_out]
    if cast_w:
        wbf_ref = refs[2 + n_extra + n_out]

        @pl.when(pl.program_id(1) == 0)
        def _():
            wbf_ref[...] = w_ref[...].astype(BF16)

        w = wbf_ref[...]
    else:
        w = w_ref[...]
    acc = _dot_nt(a_ref[...], w) if w_t else _dot(a_ref[...], w)
    vals = epilogue(acc, *[e[...] for e in extra])
    for o, v in zip(outs, vals):
        o[...] = v.astype(o.dtype)


def _matmul(name, a, w, *, col0, ncols, tm, tn, out_dtypes, epilogue, extras=(), extra_col0=(), w_t=False):
    m, k = a.shape
    assert m % tm == 0 and ncols % tn == 0 and col0 % tn == 0
    nj, ni = ncols // tn, m // tm
    jb = col0 // tn
    cast_w = w.dtype != BF16
    w_block = (tn, k) if w_t else (k, tn)
    w_map = (lambda j, i: (j + jb, 0)) if w_t else (lambda j, i: (0, j + jb))
    in_specs = [pl.BlockSpec((tm, k), lambda j, i: (i, 0)), pl.BlockSpec(w_block, w_map)]
    block_bytes = _nbytes((tm, k), a.dtype) + _nbytes(w_block, w.dtype)
    for e, c0 in zip(extras, extra_col0):
        assert c0 % tn == 0
        eb = c0 // tn
        in_specs.append(pl.BlockSpec((tm, tn), lambda j, i, eb=eb: (i, j + eb)))
        block_bytes += _nbytes((tm, tn), e.dtype)
    out_shape = [jax.ShapeDtypeStruct((m, ncols), d) for d in out_dtypes]
    out_specs = [pl.BlockSpec((tm, tn), lambda j, i: (i, j)) for _ in out_dtypes]
    for d in out_dtypes:
        block_bytes += _nbytes((tm, tn), d)
    scratch = [pltpu.VMEM(w_block, BF16)] if cast_w else []
    scratch_bytes = _nbytes(w_block, BF16) if cast_w else 0
    kern = functools.partial(_mm_kernel, n_extra=len(extras), n_out=len(out_dtypes),
                             epilogue=epilogue, cast_w=cast_w, w_t=w_t)
    return pl.pallas_call(
        kern,
        name=name,
        grid=(nj, ni),
        in_specs=in_specs,
        out_specs=out_specs,
        out_shape=out_shape,
        scratch_shapes=scratch,
        compiler_params=pltpu.CompilerParams(
            dimension_semantics=("arbitrary", "arbitrary"),
            vmem_limit_bytes=_vmem_limit(block_bytes, scratch_bytes, 2 * _nbytes((tm, tn), F32))),
    )(a, w, *extras)


def _ep_identity(acc):
    return (acc,)


def _ep_dup(acc):
    return (acc, acc)


def _ep_sigmoid(acc):
    return (jax.nn.sigmoid(acc),)


def _ep_residual(acc, res):
    return (ALPHA * res + acc,)


def _merge_kernel(oa_ref, ob_ref, wa_ref, wb_ref, ga_ref, gb_ref, out_ref, wabf, wbbf):
    @pl.when(pl.program_id(1) == 0)
    def _():
        wabf[...] = wa_ref[...].astype(BF16)
        wbbf[...] = wb_ref[...].astype(BF16)

    ba = _dot(oa_ref[...], wabf[...])
    bb = _dot(ob_ref[...], wbbf[...])
    out_ref[...] = (ga_ref[...].astype(F32) * ba + gb_ref[...].astype(F32) * bb).astype(out_ref.dtype)


def _merge(name, oa, ob, wa, wb, gates, *, tm, tn):
    m, k = oa.shape
    n = wa.shape[1]
    nj, ni = n // tn, m // tm
    gb0 = n // tn
    block_bytes = (2 * _nbytes((tm, k), BF16) + 2 * _nbytes((k, tn), F32)
                   + 2 * _nbytes((tm, tn), gates.dtype) + _nbytes((tm, tn), BF16))
    return pl.pallas_call(
        _merge_kernel,
        name=name,
        grid=(nj, ni),
        in_specs=[pl.BlockSpec((tm, k), lambda j, i: (i, 0)),
                  pl.BlockSpec((tm, k), lambda j, i: (i, 0)),
                  pl.BlockSpec((k, tn), lambda j, i: (0, j)),
                  pl.BlockSpec((k, tn), lambda j, i: (0, j)),
                  pl.BlockSpec((tm, tn), lambda j, i: (i, j)),
                  pl.BlockSpec((tm, tn), lambda j, i: (i, j + gb0))],
        out_specs=pl.BlockSpec((tm, tn), lambda j, i: (i, j)),
        out_shape=jax.ShapeDtypeStruct((m, n), BF16),
        scratch_shapes=[pltpu.VMEM((k, tn), BF16), pltpu.VMEM((k, tn), BF16)],
        compiler_params=pltpu.CompilerParams(
            dimension_semantics=("arbitrary", "arbitrary"),
            vmem_limit_bytes=_vmem_limit(block_bytes, 2 * _nbytes((k, tn), BF16),
                                         4 * _nbytes((tm, tn), F32))),
    )(oa, ob, wa, wb, gates, gates)


def _ln_kernel(x_ref, g_ref, b_ref, *out_refs):
    x = x_ref[...]
    mu = jnp.mean(x, axis=-1, keepdims=True)
    xc = x - mu
    var = jnp.mean(xc * xc, axis=-1, keepdims=True)
    y = xc * lax.rsqrt(var + LN_EPS) * g_ref[...] + b_ref[...]
    for o in out_refs:
        o[...] = y.astype(o.dtype)


def _layer_norm(name, x, g, b, *, tm, out_dtypes):
    m, d = x.shape
    block_bytes = _nbytes((tm, d), F32) + sum(_nbytes((tm, d), t) for t in out_dtypes)
    return pl.pallas_call(
        _ln_kernel,
        name=name,
        grid=(m // tm,),
        in_specs=[pl.BlockSpec((tm, d), lambda i: (i, 0)),
                  pl.BlockSpec((1, d), lambda i: (0, 0)),
                  pl.BlockSpec((1, d), lambda i: (0, 0))],
        out_specs=[pl.BlockSpec((tm, d), lambda i: (i, 0)) for _ in out_dtypes],
        out_shape=[jax.ShapeDtypeStruct((m, d), t) for t in out_dtypes],
        compiler_params=pltpu.CompilerParams(
            dimension_semantics=("arbitrary",),
            vmem_limit_bytes=_vmem_limit(block_bytes, 0, 3 * _nbytes((tm, d), F32))),
    )(x, g.reshape(1, d), b.reshape(1, d))


def _log_sigmoid(x):
    return jnp.minimum(x, 0.0) - jnp.log1p(jnp.exp(-jnp.abs(x)))


def _split3(x):
    hi = x.astype(BF16)
    r1 = x - hi.astype(F32)
    mid = r1.astype(BF16)
    lo = (r1 - mid.astype(F32)).astype(BF16)
    return hi, mid, lo


def _dot_exact01(x, ones_bf16):
    hi, mid, lo = _split3(x)
    return _dot(hi, ones_bf16) + _dot(mid, ones_bf16) + _dot(lo, ones_bf16)


def _fgate_kernel(x_ref, wf_ref, wft_ref, bf_row_ref, bf_col_ref, u_ref,
                  logf_ref, csum_ref, carry_ref, *, carry_tiles):
    x = x_ref[...]
    f_nat = _dot(x, wf_ref[...])
    logf_ref[...] = _log_sigmoid(f_nat[:, :N_HEADS] + bf_row_ref[...])
    f_t = _dot_nt(wft_ref[...], x)
    logf_t = _log_sigmoid(f_t + bf_col_ref[...])
    c = _dot_exact01(logf_t, u_ref[...])
    if carry_tiles:
        t = pl.program_id(1)

        @pl.when(t == 0)
        def _():
            carry_ref[...] = jnp.zeros_like(carry_ref)

        c = c + carry_ref[:, 0:1]
        carry_ref[...] = jnp.broadcast_to(c[:, -1:], carry_ref.shape)
    csum_ref[...] = c


def _fgate(name, x_bf, w_f_t, b_f, u_mat, *, n_seq, tiles_per_seq, tm, carry_tiles):
    m, k = x_bf.shape
    wft = w_f_t.astype(BF16)
    wf_pad = jnp.zeros((k, LANES), BF16).at[:, :N_HEADS].set(wft.T)
    bf_row = b_f.reshape(1, N_HEADS)
    bf_col = b_f.reshape(N_HEADS, 1)
    kern = functools.partial(_fgate_kernel, carry_tiles=carry_tiles)
    tps = tiles_per_seq
    block_bytes = (_nbytes((tm, k), BF16) + _nbytes((k, LANES), BF16) + _nbytes((N_HEADS, k), BF16)
                   + _nbytes((tm, tm), BF16) + _nbytes((tm, LANES), F32) + _nbytes((N_HEADS, tm), F32))
    return pl.pallas_call(
        kern,
        name=name,
        grid=(n_seq, tps),
        in_specs=[pl.BlockSpec((tm, k), lambda n, t: (n * tps + t, 0)),
                  pl.BlockSpec((k, LANES), lambda n, t: (0, 0)),
                  pl.BlockSpec((N_HEADS, k), lambda n, t: (0, 0)),
                  pl.BlockSpec((1, N_HEADS), lambda n, t: (0, 0)),
                  pl.BlockSpec((N_HEADS, 1), lambda n, t: (0, 0)),
                  pl.BlockSpec((tm, tm), lambda n, t: (0, 0))],
        out_specs=[pl.BlockSpec((tm, N_HEADS), lambda n, t: (n * tps + t, 0)),
                   pl.BlockSpec((N_HEADS, tm), lambda n, t: (0, n * tps + t))],
        out_shape=[jax.ShapeDtypeStruct((m, N_HEADS), F32),
                   jax.ShapeDtypeStruct((N_HEADS, m), F32)],
        scratch_shapes=[pltpu.VMEM((N_HEADS, LANES), F32)],
        compiler_params=pltpu.CompilerParams(
            dimension_semantics=("arbitrary", "arbitrary"),
            vmem_limit_bytes=_vmem_limit(block_bytes, 0, 8 * _nbytes((tm, LANES), F32))),
    )(x_bf, wf_pad, wft, bf_row, bf_col, u_mat)


def _suffix_kernel(x_ref, m_ref, o_ref):
    o_ref[...] = _dot_exact01(x_ref[...], m_ref[...])


def _page_suffix_sums(logf_pool):
    n_pool = logf_pool.shape[0]
    rows = n_pool * N_HEADS
    x = jnp.transpose(logf_pool, (0, 2, 1)).reshape(rows, PAGE)
    sel = (jnp.arange(PAGE)[:, None] >= jnp.arange(PAGE)[None, :]).astype(BF16)
    tr = 4096
    assert rows % tr == 0
    block_bytes = 2 * _nbytes((tr, PAGE), F32) + _nbytes((PAGE, PAGE), BF16)
    out = pl.pallas_call(
        _suffix_kernel,
        name="page_suffix_sums",
        grid=(rows // tr,),
        in_specs=[pl.BlockSpec((tr, PAGE), lambda i: (i, 0)),
                  pl.BlockSpec((PAGE, PAGE), lambda i: (0, 0))],
        out_specs=pl.BlockSpec((tr, PAGE), lambda i: (i, 0)),
        out_shape=jax.ShapeDtypeStruct((rows, PAGE), F32),
        compiler_params=pltpu.CompilerParams(
            dimension_semantics=("arbitrary",),
            vmem_limit_bytes=_vmem_limit(block_bytes, 0, 6 * _nbytes((tr, PAGE), F32))),
    )(x, sel)
    return out.reshape(n_pool, N_HEADS, PAGE)


TQ = 256
NQB = SEQ // TQ


def _lambda_value(lq1, lk1, lq2, lk2):
    s1 = jnp.sum(lq1[...] * lk1[...], axis=-1, keepdims=True)
    s2 = jnp.sum(lq2[...] * lk2[...], axis=-1, keepdims=True)
    return jnp.exp(s1) - jnp.exp(s2) + LAM_INIT


def _causal_strip():
    r = lax.broadcasted_iota(jnp.int32, (TQ, SEQ), 0)
    j = lax.broadcasted_iota(jnp.int32, (TQ, SEQ), 1)
    return r - j + (SEQ - TQ)


def _diff_prompt_kernel(lq1, lk1, lq2, lk2, g_ref, q_ref, k_ref, v_ref, o_ref, bias_ref):
    h = pl.program_id(1)
    lam = _lambda_value(lq1, lk1, lq2, lk2)
    slope = jnp.exp2(jnp.full((1, 1), -0.5, F32) * (h + 1).astype(F32))
    d = _causal_strip()
    bias_ref[...] = jnp.where(d >= 0, -slope * d.astype(F32), NEG_INF)
    lane = lax.broadcasted_iota(jnp.int32, (TQ, HEAD_W), 1)
    g = g_ref[...]
    for b in range(NQB):
        s_len = (b + 1) * TQ
        q = q_ref[b * TQ:(b + 1) * TQ, :] * jnp.asarray(DK_A ** -0.5, BF16)
        q1 = jnp.where(lane < DK_A, q, jnp.zeros_like(q))
        q2 = jnp.where(lane >= DK_A, q, jnp.zeros_like(q))
        k = k_ref[0:s_len, :]
        v = v_ref[0:s_len, :]
        c0 = (NQB - 1 - b) * TQ
        bias = bias_ref[:, c0:c0 + s_len]
        s = _dot_nt(jnp.concatenate([q1, q2], axis=0), k) + jnp.concatenate([bias, bias], axis=0)
        m = jnp.max(s, axis=-1, keepdims=True)
        e = jnp.exp(s - m)
        l = jnp.sum(e, axis=-1, keepdims=True)
        pv = _dot(e.astype(BF16), v) / l
        o = pv[0:TQ] - lam * pv[TQ:2 * TQ]
        o = o * lax.rsqrt(jnp.mean(o * o, axis=-1, keepdims=True) + RMS_EPS) * g
        o_ref[b * TQ:(b + 1) * TQ, :] = (o * (1.0 - LAM_INIT)).astype(o_ref.dtype)


def _fox_prompt_kernel(q_ref, k_ref, v_ref, c_ref, o_ref, mask_ref):
    d = _causal_strip()
    mask_ref[...] = jnp.where(d >= 0, 0.0, NEG_INF).astype(F32)
    scale = HEAD_W ** -0.5
    for b in range(NQB):
        s_len = (b + 1) * TQ
        q = q_ref[b * TQ:(b + 1) * TQ, :]
        k = k_ref[0:s_len, :]
        v = v_ref[0:s_len, :]
        c0 = (NQB - 1 - b) * TQ
        bias = mask_ref[:, c0:c0 + s_len] - c_ref[:, 0:s_len]
        s = _dot_nt(q, k) * scale + bias
        m = jnp.max(s, axis=-1, keepdims=True)
        e = jnp.exp(s - m)
        l = jnp.sum(e, axis=-1, keepdims=True)
        o_ref[b * TQ:(b + 1) * TQ, :] = (_dot(e.astype(BF16), v) / l).astype(o_ref.dtype)


def _head_spec():
    return pl.BlockSpec((SEQ, HEAD_W), lambda n, h: (n, h))


def _small_spec(shape):
    return pl.BlockSpec(shape, lambda n, h: (0,) * len(shape))


_ATT_TEMP = (4 * 4 + 2) * 2 * TQ * SEQ


def _diff_prompt(q, k, v, lams, subln_g, n_batch):
    block_bytes = 4 * _nbytes((SEQ, HEAD_W), BF16)
    return pl.pallas_call(
        _diff_prompt_kernel,
        name="diff_prompt_attention",
        grid=(n_batch, N_HEADS),
        in_specs=[_small_spec((1, DK_A))] * 4 + [_small_spec((1, HEAD_W))] + [_head_spec()] * 3,
        out_specs=_head_spec(),
        out_shape=jax.ShapeDtypeStruct(q.shape, BF16),
        scratch_shapes=[pltpu.VMEM((TQ, SEQ), F32)],
        compiler_params=pltpu.CompilerParams(
            dimension_semantics=("arbitrary", "arbitrary"),
            vmem_limit_bytes=_vmem_limit(block_bytes, _nbytes((TQ, SEQ), F32), _ATT_TEMP)),
    )(*lams, subln_g.reshape(1, HEAD_W), q, k, v)


def _fox_prompt(q, k, v, csum, n_batch):
    block_bytes = 4 * _nbytes((SEQ, HEAD_W), BF16) + _nbytes((8, SEQ), F32)
    return pl.pallas_call(
        _fox_prompt_kernel,
        name="fox_prompt_attention",
        grid=(n_batch, N_HEADS),
        in_specs=[_head_spec()] * 3 + [pl.BlockSpec((None, None, 1, SEQ), lambda n, h: (n, h, 0, 0))],
        out_specs=_head_spec(),
        out_shape=jax.ShapeDtypeStruct(q.shape, BF16),
        scratch_shapes=[pltpu.VMEM((TQ, SEQ), F32)],
        compiler_params=pltpu.CompilerParams(
            dimension_semantics=("arbitrary", "arbitrary"),
            vmem_limit_bytes=_vmem_limit(block_bytes, _nbytes((TQ, SEQ), F32), _ATT_TEMP)),
    )(q, k, v, csum)


N_PAGES = SEQ // PAGE
PPS = 4
N_STEPS = N_PAGES // PPS
DEC_T = 4
ROWS = N_HEADS * SUBLANES
HEAD_GROUPS = N_HEADS // SUBLANES


def _expand_heads(x):
    return jnp.concatenate(
        [jnp.broadcast_to(x[h:h + 1, :], (SUBLANES, x.shape[1])) for h in range(N_HEADS)], axis=0)


def _head_rows(x, h):
    return x[h * SUBLANES:(h + 1) * SUBLANES, :]


def _online_update(s, v_of_head, m_ref, l_ref, acc_ref):
    m_old = m_ref[...]
    m_new = jnp.maximum(m_old, jnp.max(s, axis=-1, keepdims=True))
    e = jnp.exp(s - m_new)
    alpha = jnp.exp(m_old - m_new)
    l_ref[...] = alpha * l_ref[...] + jnp.sum(e, axis=-1, keepdims=True)
    pv = jnp.concatenate([_dot(_head_rows(e, h), v_of_head(h)) for h in range(N_HEADS)], axis=0)
    acc_ref[...] = alpha * acc_ref[...] + pv
    m_ref[...] = m_new


def _decode_kernel(pt_ref, lq1, lk1, lq2, lk2, g_ref, qa_ref, qb_ref, *rest):
    del pt_ref
    n_blk = PPS * HEAD_GROUPS
    kta = rest[0:PPS]
    va, kb, vb = (rest[PPS + i * n_blk:PPS + (i + 1) * n_blk] for i in range(3))
    suf = rest[PPS + 3 * n_blk:2 * PPS + 3 * n_blk]
    (kna_ref, vna_ref, knb_ref, vnb_ref, cnew_ref, oa_ref, ob_ref,
     ma, la, acca, mb, lb, accb, carry, pka, pva, pkb, pvb) = rest[2 * PPS + 3 * n_blk:]
    b = pl.program_id(0)
    step = pl.program_id(1)
    row = lax.broadcasted_iota(jnp.int32, (ROWS, LANES), 0)
    lane = lax.broadcasted_iota(jnp.int32, (ROWS, LANES), 1)
    head_f = (row // SUBLANES + 1).astype(F32)
    t_row = row % DEC_T
    slope = jnp.exp2(-0.5 * head_f)
    fox_scale = HEAD_W ** -0.5

    @pl.when((b == 0) & (step == 0))
    def _():
        for p in (pka, pva, pkb, pvb):
            p[...] = jnp.zeros_like(p)

    @pl.when(step == 0)
    def _():
        for r in (ma, mb):
            r[...] = jnp.full_like(r, NEG_INF)
        for r in (la, lb, acca, accb, carry):
            r[...] = jnp.zeros_like(r)

    def strided_head(refs):
        def load(h):
            hg, h8 = divmod(h, SUBLANES)
            return jnp.concatenate(
                [refs[g * HEAD_GROUPS + hg].reshape(PAGE * SUBLANES, HEAD_W)[pl.ds(h8, PAGE, stride=SUBLANES), :]
                 for g in range(PPS)], axis=0)
        return load

    def cached_pages_step():
        q_pos = (SEQ + t_row).astype(F32)
        bias_a = []
        for g in range(PPS):
            page = N_PAGES - 1 - (step * PPS + g)
            key_pos = (page * PAGE).astype(F32) + lane.astype(F32)
            bias_a.append(-slope * (q_pos - key_pos))
        sa = jnp.concatenate(
            [_dot(qa_ref[h * SUBLANES:(h + 1) * SUBLANES, :],
                  jnp.concatenate([r[h * HEAD_W:(h + 1) * HEAD_W, :] for r in kta], axis=1))
             for h in range(N_HEADS)], axis=0)
        sa = sa + jnp.concatenate(bias_a, axis=1)
        _online_update(sa, strided_head(va), ma, la, acca)

        lane16 = lax.broadcasted_iota(jnp.int32, (N_HEADS, LANES), 1)
        c = carry[...]
        bias_b = []
        for g in range(PPS):
            sfx = suf[g][...]
            excl = jnp.where(lane16 == LANES - 1, 0.0, pltpu.roll(sfx, LANES - 1, axis=1))
            bias_b.append(_expand_heads(excl) + c)
            c = c + _expand_heads(jnp.broadcast_to(sfx[:, 0:1], (N_HEADS, LANES)))
        carry[...] = c
        k_of_head = strided_head(kb)
        sb = jnp.concatenate(
            [_dot_nt(qb_ref[h * SUBLANES:(h + 1) * SUBLANES, :], k_of_head(h)) for h in range(N_HEADS)],
            axis=0)
        sb = sb * fox_scale + jnp.concatenate(bias_b, axis=1)
        _online_update(sb, strided_head(vb), mb, lb, accb)

    def head_slab(ref):
        return lambda h: ref[:, h * HEAD_W:(h + 1) * HEAD_W]

    def new_logits(q_ref, k_ref):
        k_of_head = head_slab(k_ref)
        return jnp.concatenate(
            [_dot_nt(q_ref[h * SUBLANES:(h + 1) * SUBLANES, :], k_of_head(h)) for h in range(N_HEADS)],
            axis=0)

    cached_pages_step()

    @pl.when(step == N_STEPS - 1)
    def _():
        pka[0:DEC_T, :] = kna_ref[...]
        pva[0:DEC_T, :] = vna_ref[...]
        pkb[0:DEC_T, :] = knb_ref[...]
        pvb[0:DEC_T, :] = vnb_ref[...]
        visible = (lane <= t_row) & (lane < DEC_T)
        bias_a = jnp.where(visible, -slope * (t_row - lane).astype(F32), NEG_INF)
        _online_update(new_logits(qa_ref, pka) + bias_a, head_slab(pva), ma, la, acca)
        bias_b = jnp.where(visible, -_expand_heads(cnew_ref[...]), NEG_INF)
        _online_update(new_logits(qb_ref, pkb) * fox_scale + bias_b, head_slab(pvb), mb, lb, accb)

        lam = _lambda_value(lq1, lk1, lq2, lk2)
        na = acca[...] / la[...]
        o = na - lam * pltpu.roll(na, ROWS - DEC_T, axis=0)
        o = o * lax.rsqrt(jnp.mean(o * o, axis=-1, keepdims=True) + RMS_EPS) * g_ref[...]
        o = o * (1.0 - LAM_INIT)
        nb = accb[...] / lb[...]
        for h in range(N_HEADS):
            oa_ref[:, h * HEAD_W:(h + 1) * HEAD_W] = o[h * SUBLANES:h * SUBLANES + DEC_T, :]
            ob_ref[:, h * HEAD_W:(h + 1) * HEAD_W] = nb[h * SUBLANES:h * SUBLANES + DEC_T, :]


def _decode_attention(page_table, lams, subln_g, qa8, qb8, pools, suffix, new_kv, cnew):
    n_seq = page_table.shape[0]

    def page_of(b, s, pt, g):
        return pt[b, N_PAGES - 1 - (s * PPS + g)]

    def page_specs(shape):
        return [pl.BlockSpec((None,) + shape, lambda b, s, pt, g=g: (page_of(b, s, pt, g), 0, 0))
                for g in range(PPS)]

    def head_group_specs():
        return [pl.BlockSpec((None, PAGE, None, SUBLANES, HEAD_W),
                             lambda b, s, pt, g=g, hg=hg: (page_of(b, s, pt, g), 0, hg, 0, 0))
                for g in range(PPS) for hg in range(HEAD_GROUPS)]

    def seq_map(b, s, pt):
        return (b, 0, 0)

    def const_map(b, s, pt):
        return (0, 0)

    in_specs = ([pl.BlockSpec((1, DK_A), const_map)] * 4 + [pl.BlockSpec((1, HEAD_W), const_map)]
                + [pl.BlockSpec((None, ROWS, HEAD_W), seq_map)] * 2
                + page_specs((WIDTH, PAGE))
                + head_group_specs() * 3
                + page_specs((N_HEADS, PAGE))
                + [pl.BlockSpec((None, DEC_T, WIDTH), seq_map)] * 4
                + [pl.BlockSpec((None, N_HEADS, LANES), seq_map)])
    out_spec = pl.BlockSpec((None, DEC_T, WIDTH), seq_map)
    scratch = ([pltpu.VMEM((ROWS, 1), F32), pltpu.VMEM((ROWS, 1), F32), pltpu.VMEM((ROWS, HEAD_W), F32)] * 2
               + [pltpu.VMEM((ROWS, LANES), F32)]
               + [pltpu.VMEM((PAGE, WIDTH), F32)] * 4)
    block_bytes = 4 * PPS * _nbytes((PAGE, WIDTH), F32) + 12 * _nbytes((ROWS, LANES), F32)
    scratch_bytes = 4 * _nbytes((PAGE, WIDTH), F32) + 8 * _nbytes((ROWS, LANES), F32)
    page_args = ([pools[0]] * PPS + [p for pool in pools[1:] for p in [pool] * (PPS * HEAD_GROUPS)]
                 + [suffix] * PPS)
    return pl.pallas_call(
        _decode_kernel,
        name="paged_decode_attention",
        grid_spec=pltpu.PrefetchScalarGridSpec(
            num_scalar_prefetch=1,
            grid=(n_seq, N_STEPS),
            in_specs=in_specs,
            out_specs=[out_spec, out_spec],
            scratch_shapes=scratch),
        out_shape=[jax.ShapeDtypeStruct((n_seq, DEC_T, WIDTH), F32)] * 2,
        compiler_params=pltpu.CompilerParams(
            dimension_semantics=("arbitrary", "arbitrary"),
            vmem_limit_bytes=_vmem_limit(block_bytes, scratch_bytes,
                                         256 * _nbytes((ROWS, LANES), F32))),
    )(page_table, *lams, subln_g.reshape(1, HEAD_W), qa8, qb8, *page_args, *new_kv, cnew)


TN_FF = 256


def _conv_gate(a, r1, r2, b, cw, cb):
    conv = cb + (cw[0:1] * r2 + cw[1:2] * r1 + cw[2:3] * a)
    return jax.nn.silu(conv) * b


def _cast_weights(wa_ref, wb_ref, wabf, wbbf):
    @pl.when(pl.program_id(1) == 0)
    def _():
        wabf[...] = wa_ref[...].astype(BF16)
        wbbf[...] = wb_ref[...].astype(BF16)


def _ffn_up_prompt_kernel(h_ref, wa_ref, wb_ref, cw_ref, cb_ref, act_ref, tail_ref,
                          wabf, wbbf, prev_ref, *, tiles_per_seq):
    _cast_weights(wa_ref, wb_ref, wabf, wbbf)
    t = pl.program_id(1) % tiles_per_seq
    hv = h_ref[...]
    a = _dot(hv, wabf[...])
    b = _dot(hv, wbbf[...])
    tm = a.shape[0]
    prev = jnp.where(t == 0, 0.0, prev_ref[...])
    row8 = lax.broadcasted_iota(jnp.int32, (SUBLANES, a.shape[1]), 0)
    r1 = pltpu.roll(a, 1, axis=0)
    r2 = pltpu.roll(a, 2, axis=0)
    r1_first = jnp.where(row8 < 1, pltpu.roll(prev, 1, axis=0), r1[0:SUBLANES])
    r2_first = jnp.where(row8 < 2, pltpu.roll(prev, 2, axis=0), r2[0:SUBLANES])
    r1 = jnp.concatenate([r1_first, r1[SUBLANES:tm]], axis=0)
    r2 = jnp.concatenate([r2_first, r2[SUBLANES:tm]], axis=0)
    act_ref[...] = _conv_gate(a, r1, r2, b, cw_ref[...], cb_ref[...]).astype(act_ref.dtype)
    last = a[tm - SUBLANES:tm]
    prev_ref[...] = last
    tail_ref[...] = last


def _ffn_up_sample_kernel(h_ref, wa_ref, wb_ref, cw_ref, cb_ref, s0_ref, s1_ref, act_ref, a_ref,
                          wabf, wbbf):
    _cast_weights(wa_ref, wb_ref, wabf, wbbf)
    hv = h_ref[...]
    a = _dot(hv, wabf[...])
    b = _dot(hv, wbbf[...])
    t = lax.broadcasted_iota(jnp.int32, a.shape, 0) % DEC_T
    s0, s1 = s0_ref[...], s1_ref[...]
    r1 = jnp.where(t == 0, s1, pltpu.roll(a, 1, axis=0))
    r2 = jnp.where(t == 0, s0, jnp.where(t == 1, s1, pltpu.roll(a, 2, axis=0)))
    act_ref[...] = _conv_gate(a, r1, r2, b, cw_ref[...], cb_ref[...]).astype(act_ref.dtype)
    a_ref[...] = a


def _ffn_up_specs(tm, k, nj):
    tn = TN_FF
    return [pl.BlockSpec((tm, k), lambda j, i: (i, 0)),
            pl.BlockSpec((k, tn), lambda j, i: (0, j)),
            pl.BlockSpec((k, tn), lambda j, i: (0, j + nj)),
            pl.BlockSpec((3, tn), lambda j, i: (0, j)),
            pl.BlockSpec((1, tn), lambda j, i: (0, j))]


def _ffn_up_prompt(h_bf, w_up, conv_w, conv_b, *, tm):
    m, k = h_bf.shape
    tn = TN_FF
    nj, ni, tps = D_FF // tn, m // tm, SEQ // tm
    kern = functools.partial(_ffn_up_prompt_kernel, tiles_per_seq=tps)
    block_bytes = (_nbytes((tm, k), BF16) + 2 * _nbytes((k, tn), F32) + _nbytes((tm, tn), BF16)
                   + 5 * _nbytes((SUBLANES, tn), F32))
    scratch_bytes = 2 * _nbytes((k, tn), BF16) + _nbytes((SUBLANES, tn), F32)
    return pl.pallas_call(
        kern,
        name="ffn_up_conv_prompt",
        grid=(nj, ni),
        in_specs=_ffn_up_specs(tm, k, nj),
        out_specs=[pl.BlockSpec((tm, tn), lambda j, i: (i, j)),
                   pl.BlockSpec((SUBLANES, tn), lambda j, i: (i // tps, j))],
        out_shape=[jax.ShapeDtypeStruct((m, D_FF), BF16),
                   jax.ShapeDtypeStruct((m // SEQ * SUBLANES, D_FF), F32)],
        scratch_shapes=[pltpu.VMEM((k, tn), BF16), pltpu.VMEM((k, tn), BF16),
                        pltpu.VMEM((SUBLANES, tn), F32)],
        compiler_params=pltpu.CompilerParams(
            dimension_semantics=("arbitrary", "arbitrary"),
            vmem_limit_bytes=_vmem_limit(block_bytes, scratch_bytes, 8 * _nbytes((tm, tn), F32))),
    )(h_bf, w_up, w_up, conv_w, conv_b.reshape(1, D_FF))


def _ffn_up_sample(h_bf, w_up, conv_w, conv_b, state):
    m, k = h_bf.shape
    tn = TN_FF
    nj = D_FF // tn
    s0 = jnp.repeat(state[:, 0], DEC_T, axis=0)
    s1 = jnp.repeat(state[:, 1], DEC_T, axis=0)
    block_bytes = (_nbytes((m, k), BF16) + 2 * _nbytes((k, tn), F32) + _nbytes((m, tn), BF16)
                   + 3 * _nbytes((m, tn), F32))
    scratch_bytes = 2 * _nbytes((k, tn), BF16)
    tile = pl.BlockSpec((m, tn), lambda j, i: (0, j))
    return pl.pallas_call(
        _ffn_up_sample_kernel,
        name="ffn_up_conv_sample",
        grid=(nj, 1),
        in_specs=_ffn_up_specs(m, k, nj) + [tile, tile],
        out_specs=[tile, tile],
        out_shape=[jax.ShapeDtypeStruct((m, D_FF), BF16), jax.ShapeDtypeStruct((m, D_FF), F32)],
        scratch_shapes=[pltpu.VMEM((k, tn), BF16), pltpu.VMEM((k, tn), BF16)],
        compiler_params=pltpu.CompilerParams(
            dimension_semantics=("arbitrary", "arbitrary"),
            vmem_limit_bytes=_vmem_limit(block_bytes, scratch_bytes, 8 * _nbytes((m, tn), F32))),
    )(h_bf, w_up, w_up, conv_w, conv_b.reshape(1, D_FF), s0, s1)


TN = 512
TN_PROJ = 1024


def _trunk(tag, x_f32, attn_a, attn_b, gates, w, *, tm, ffn_up_fn):
    gated = _merge("merge_" + tag, attn_a, attn_b, w["w_proj_a"], w["w_proj_b"], gates, tm=tm, tn=TN)
    (s1,) = _matmul("out_proj_" + tag, gated, w["w_out"], col0=0, ncols=D_MODEL, tm=tm, tn=TN,
                    out_dtypes=(F32,), epilogue=_ep_residual, extras=(x_f32,), extra_col0=(0,))
    h_f32, h_bf = _layer_norm("ln1_" + tag, s1, w["ln1_g"], w["ln1_b"], tm=256, out_dtypes=(F32, BF16))
    act, a_info = ffn_up_fn(h_bf)
    (s2,) = _matmul("ffn_down_" + tag, act, w["w_ffn_down_bf"], col0=0, ncols=D_MODEL,
                    tm=min(tm, 512), tn=TN, out_dtypes=(F32,), epilogue=_ep_residual,
                    extras=(h_f32,), extra_col0=(0,))
    (y,) = _layer_norm("ln2_" + tag, s2, w["ln2_g"], w["ln2_b"], tm=256, out_dtypes=(F32,))
    return y, a_info


def kernel(x_prompt, x_sample, cache_diff_k, cache_diff_v, cache_fox_k, cache_fox_v, cache_fox_logf,
           state_ffn_conv, page_table, w_in, b_f, lambda_q1, lambda_k1, lambda_q2, lambda_k2, subln_g,
           w_proj_a, w_proj_b, w_out, ln1_g, ln1_b, w_ffn_up, conv_w, conv_b, w_ffn_down, ln2_g, ln2_b):
    n_batch, seq, d_model = x_prompt.shape
    n_seq, dec_t, _ = x_sample.shape
    assert (seq, d_model, dec_t) == (SEQ, D_MODEL, DEC_T) and w_in.shape[0] == DEPTH == 1
    mp, ms = n_batch * seq, n_seq * dec_t

    w_in_t = jnp.transpose(w_in[0], (1, 0))
    off_f = 6 * WIDTH
    off_g = off_f + N_HEADS
    w_f_t = w_in_t[off_f:off_g]
    w_gate_t = w_in_t[off_g:]
    w = dict(w_proj_a=w_proj_a[0], w_proj_b=w_proj_b[0], w_out=w_out[0], ln1_g=ln1_g[0], ln1_b=ln1_b[0],
             w_ffn_down_bf=w_ffn_down[0].astype(BF16), ln2_g=ln2_g[0], ln2_b=ln2_b[0])
    w_up = w_ffn_up[0]
    lams = tuple(v.reshape(1, DK_A) for v in (lambda_q1[0], lambda_k1[0], lambda_q2[0], lambda_k2[0]))

    xp = x_prompt.reshape(mp, d_model)
    xs = x_sample.reshape(ms, d_model)
    xp_bf, xs_bf = xp.astype(BF16), xs.astype(BF16)

    def project(tag, x_bf, tm, with_bf_copy):
        res = {}
        names = ("qa", "ka", "va", "qb", "kb", "vb")
        for idx, name in enumerate(names):
            common = dict(col0=idx * WIDTH, ncols=WIDTH, tm=tm, tn=TN_PROJ, w_t=True)
            call = "proj_%s_%s" % (name, tag)
            if name[0] == "q":
                (res[name],) = _matmul(call, x_bf, w_in_t, out_dtypes=(BF16,), epilogue=_ep_identity, **common)
            elif with_bf_copy:
                res[name], res[name + "_bf"] = _matmul(call, x_bf, w_in_t, out_dtypes=(F32, BF16),
                                                       epilogue=_ep_dup, **common)
            else:
                (res[name],) = _matmul(call, x_bf, w_in_t, out_dtypes=(F32,), epilogue=_ep_identity, **common)
        (res["gates"],) = _matmul("proj_gates_" + tag, x_bf, w_gate_t, col0=0, ncols=2 * D_MODEL, tm=tm,
                                  tn=TN_PROJ, out_dtypes=(BF16,), epilogue=_ep_sigmoid, w_t=True)
        return res

    pp = project("prompt", xp_bf, 512, True)
    ps = project("sample", xs_bf, ms, False)

    tmf = 512
    tri = (jnp.arange(tmf)[:, None] <= jnp.arange(tmf)[None, :])
    logf_p, csum_p = _fgate("fgate_prompt", xp_bf, w_f_t, b_f[0], tri.astype(BF16), n_seq=n_batch,
                            tiles_per_seq=seq // tmf, tm=tmf, carry_tiles=True)
    grp = jnp.arange(ms) // dec_t
    blockdiag = tri[:ms, :ms] & (grp[:, None] == grp[None, :])
    logf_s, csum_s = _fgate("fgate_sample", xs_bf, w_f_t, b_f[0], blockdiag.astype(BF16), n_seq=1,
                            tiles_per_seq=1, tm=ms, carry_tiles=False)

    oa_p = _diff_prompt(pp["qa"], pp["ka_bf"], pp["va_bf"], lams, subln_g[0], n_batch)
    csum_rows = csum_p.reshape(N_HEADS, n_batch, 1, seq).transpose(1, 0, 2, 3)
    ob_p = _fox_prompt(pp["qb"], pp["kb_bf"], pp["vb_bf"], csum_rows, n_batch)

    n_pool = cache_diff_k.shape[1]
    kt_a = jnp.transpose(cache_diff_k[0], (0, 2, 3, 4, 1)).reshape(n_pool, WIDTH, PAGE)
    grouped = (n_pool, PAGE, HEAD_GROUPS, SUBLANES, HEAD_W)
    pools = (kt_a, cache_diff_v[0].reshape(grouped), cache_fox_k[0].reshape(grouped),
             cache_fox_v[0].reshape(grouped))
    suffix = _page_suffix_sums(cache_fox_logf[0])
    qa = ps["qa"].astype(F32).reshape(n_seq, dec_t, N_HEADS, 2, DK_A) * (DK_A ** -0.5)
    qa = qa.transpose(0, 2, 3, 1, 4)
    eye = jnp.eye(2, dtype=F32)[None, None, :, None, :, None]
    qa8 = (qa[:, :, :, :, None, :] * eye).reshape(n_seq, ROWS, HEAD_W)
    qb = ps["qb"].astype(F32).reshape(n_seq, dec_t, N_HEADS, HEAD_W).transpose(0, 2, 1, 3)
    qb8 = jnp.concatenate([qb, jnp.zeros_like(qb)], axis=2).reshape(n_seq, ROWS, HEAD_W)
    new_kv = tuple(ps[nm].reshape(n_seq, dec_t, WIDTH) for nm in ("ka", "va", "kb", "vb"))
    cnew = csum_s.reshape(N_HEADS, n_seq, dec_t).transpose(1, 0, 2)
    cnew = jnp.zeros((n_seq, N_HEADS, LANES), F32).at[:, :, :dec_t].set(cnew)
    oa_s, ob_s = _decode_attention(page_table, lams, subln_g[0], qa8, qb8, pools, suffix, new_kv, cnew)
    oa_s = oa_s.reshape(ms, WIDTH).astype(BF16)
    ob_s = ob_s.reshape(ms, WIDTH).astype(BF16)

    conv_w0, conv_b0 = conv_w[0], conv_b[0]
    y_p, tail_p = _trunk("prompt", xp, oa_p, ob_p, pp["gates"], w, tm=1024,
                         ffn_up_fn=lambda h: _ffn_up_prompt(h, w_up, conv_w0, conv_b0, tm=1024))
    y_s, a_s = _trunk("sample", xs, oa_s, ob_s, ps["gates"], w, tm=ms,
                      ffn_up_fn=lambda h: _ffn_up_sample(h, w_up, conv_w0, conv_b0, state_ffn_conv[0]))

    def heads(x, n, t, *tail):
        return x.reshape(1, n, t, N_HEADS, *tail)

    conv_p = tail_p.reshape(n_batch, SUBLANES, D_FF)[None, :, SUBLANES - 2:, :]
    conv_s = a_s.reshape(n_seq, dec_t, D_FF)[None, :, dec_t - 2:, :]
    return (y_p.reshape(n_batch, seq, d_model), y_s.reshape(n_seq, dec_t, d_model),
            heads(pp["ka"], n_batch, seq, 2, DK_A), heads(pp["va"], n_batch, seq, HEAD_W),
            heads(pp["kb"], n_batch, seq, HEAD_W), heads(pp["vb"], n_batch, seq, HEAD_W),
            heads(logf_p, n_batch, seq), conv_p,
            heads(ps["ka"], n_seq, dec_t, 2, DK_A), heads(ps["va"], n_seq, dec_t, HEAD_W),
            heads(ps["kb"], n_seq, dec_t, HEAD_W), heads(ps["vb"], n_seq, dec_t, HEAD_W),
            heads(logf_s, n_seq, dec_t), conv_s)
```
